```python
import math
import jax, jax.numpy as jnp
from jax import lax
import numpy as np

D_MODEL = 1024
BATCH = 2
SEQ = 16384
DEPTH = 4
DEC_BATCH = 16
DEC_SEQ = 32
PAST_LEN = 1024

CHUNK = 64
Q_BLOCK = 128
N_MIXERS = 4
LAYERS_PER_MIXER = DEPTH // N_MIXERS
HEAD_DIM = 128
N_HEADS = D_MODEL // HEAD_DIM
DIFF_HEADS = D_MODEL // (2 * HEAD_DIM)
DIFF_VDIM = 2 * HEAD_DIM
ROPE_THETA = 500000.0
ROPE_DIM = HEAD_DIM // 4
BAND_CHUNKS = 8
BAND_PAST = BAND_CHUNKS * CHUNK
REL_CLIP = 128
D_FF = 4 * D_MODEL
NORM_EPS = 1e-6
FORGET_BIAS_INIT = 3.0
SCALE = HEAD_DIM ** -0.5
NEG = -1e30

kernel_name = "hybrid_streaming_encoder_step"


def _rmsnorm(x, g):
    xf = x.astype(jnp.float32)
    y = xf * lax.rsqrt(jnp.mean(xf * xf, axis=-1, keepdims=True) + NORM_EPS)
    return (y * g.astype(jnp.float32)).astype(x.dtype)


def _rope(x, pos):
    half = ROPE_DIM // 2
    inv = jnp.exp(-math.log(ROPE_THETA) * jnp.arange(half, dtype=jnp.float32) * (2.0 / ROPE_DIM))
    ang = pos.astype(jnp.float32)[:, None] * inv[None, :]
    cos = jnp.cos(ang)[None, :, None, :]
    sin = jnp.sin(ang)[None, :, None, :]
    xf = x.astype(jnp.float32)
    x1, x2, rest = xf[..., :half], xf[..., half:ROPE_DIM], xf[..., ROPE_DIM:]
    out = jnp.concatenate([x1 * cos - x2 * sin, x2 * cos + x1 * sin, rest], axis=-1)
    return out.astype(x.dtype)


def _last_rows(a, n):
    s = a.shape[1]
    if s >= n:
        return a[:, s - n:]
    return jnp.pad(a, ((0, 0), (n - s, 0)) + ((0, 0),) * (a.ndim - 2))


def _causal_sweep(block_fn, q_arrays, k_arrays, pos):
    outs = []
    for i in range(pos.shape[0] // Q_BLOCK):
        lo, hi = i * Q_BLOCK, (i + 1) * Q_BLOCK
        outs.append(block_fn(tuple(a[:, lo:hi] for a in q_arrays), pos[lo:hi],
                             tuple(a[:, :hi] for a in k_arrays), pos[:hi]))
    return jnp.concatenate(outs, axis=1)


def _rev_cumsum(x):
    K = x.shape[-1]
    Kp = -(-K // Q_BLOCK) * Q_BLOCK
    xp = jnp.pad(x, [(0, 0)] * (x.ndim - 1) + [(0, Kp - K)])
    xb = xp.reshape(*x.shape[:-1], Kp // Q_BLOCK, Q_BLOCK)
    idx = jnp.arange(Q_BLOCK)
    tri = (idx[None, :] >= idx[:, None]).astype(x.dtype)
    within = jnp.einsum("...nj,ij->...ni", xb, tri, precision=lax.Precision.HIGHEST)
    tot = within[..., 0]
    off = lax.cumsum(tot, axis=tot.ndim - 1, reverse=True) - tot
    return (within + off[..., None]).reshape(*x.shape[:-1], Kp)[..., :K]


def _mlp(h, w_up, w_down):
    return jnp.square(jax.nn.relu(h @ w_up)) @ w_down


def _diff_block(q, qpos, k, v, kpos, lam):
    B, nq = q.shape[:2]
    s = jnp.einsum("bqhd,bkhd->bhqk", q, k).astype(jnp.float32)
    mask = (kpos[None, :] // CHUNK) <= (qpos[:, None] // CHUNK)
    s = jnp.where(mask[None, None], s, NEG)
    p = jnp.exp(s - jnp.max(s, axis=-1, keepdims=True))
    l = jnp.sum(p, axis=-1)
    p = p.reshape(B, DIFF_HEADS, 2, nq, p.shape[-1])
    o = jnp.einsum("bgmqk,bkge->bqgme", p.astype(v.dtype), v).astype(jnp.float32)
    l = jnp.transpose(l.reshape(B, DIFF_HEADS, 2, nq), (0, 3, 1, 2))
    o = o / l[..., None]
    return (o[:, :, :, 0] - lam * o[:, :, :, 1]).astype(v.dtype)


def _mixer_a(h, pos, past, w_in, lam, sub_g, w_out, lam_init):
    B, S, _ = h.shape
    q, k, v = jnp.split(h @ w_in, 3, axis=-1)
    q = _rope(q.reshape(B, S, 2 * DIFF_HEADS, HEAD_DIM), pos) * SCALE
    k = _rope(k.reshape(B, S, 2 * DIFF_HEADS, HEAD_DIM), pos)
    v = v.reshape(B, S, DIFF_HEADS, DIFF_VDIM)
    if past is None:
        o = _causal_sweep(lambda qs, qp, ks, kp: _diff_block(qs[0], qp, ks[0], ks[1], kp, lam),
                          (q,), (k, v), pos)
    else:
        pk, pv = past
        kk = jnp.concatenate([pk, k], axis=1)
        vv = jnp.concatenate([pv, v], axis=1)
        kpos = jnp.arange(kk.shape[1], dtype=jnp.int32)
        o = _diff_block(q, pos, kk, vv, kpos, lam)
    o = _rmsnorm(o, sub_g) * (1.0 - lam_init)
    return o.reshape(B, S, D_MODEL) @ w_out, (k, v)


def _band_block(q, qpos, k, v, kpos, rel_bias):
    s = jnp.einsum("bqhd,bkhd->bhqk", q, k).astype(jnp.float32) * SCALE
    rel = jnp.clip(qpos[:, None] - kpos[None, :], -REL_CLIP, REL_CLIP) + REL_CLIP
    s = s + rel_bias.astype(jnp.float32)[:, rel][None]
    qc, kc = qpos[:, None] // CHUNK, kpos[None, :] // CHUNK
    mask = (kpos[None, :] >= 0) & (kc <= qc) & (kc >= qc - BAND_CHUNKS)
    p = jax.nn.softmax(jnp.where(mask[None, None], s, NEG), axis=-1)
    return jnp.einsum("bhqk,bkhd->bqhd", p.astype(v.dtype), v)


def _mixer_b(h, pos, past, w_in, rel_bias, w_out):
    B, S, _ = h.shape
    q, k, v = jnp.split(h @ w_in, 3, axis=-1)
    q = q.reshape(B, S, N_HEADS, HEAD_DIM)
    k = k.reshape(B, S, N_HEADS, HEAD_DIM)
    v = v.reshape(B, S, N_HEADS, HEAD_DIM)
    wb = min(BAND_PAST, PAST_LEN)
    if past is None:
        pad = ((0, 0), (BAND_PAST, 0), (0, 0), (0, 0))
        kp, vp = jnp.pad(k, pad), jnp.pad(v, pad)
        nc = S // CHUNK
        qc = jnp.moveaxis(q.reshape(B, nc, CHUNK, N_HEADS, HEAD_DIM), 1, 0)
        pc = pos.reshape(nc, CHUNK)

        def one_chunk(args):
            qblk, qpos_c = args
            start = qpos_c[0]
            kb = lax.dynamic_slice_in_dim(kp, start, BAND_PAST + CHUNK, axis=1)
            vb = lax.dynamic_slice_in_dim(vp, start, BAND_PAST + CHUNK, axis=1)
            kpos = start - BAND_PAST + jnp.arange(BAND_PAST + CHUNK, dtype=jnp.int32)
            return _band_block(qblk, qpos_c, kb, vb, kpos, rel_bias)

        o = lax.map(one_chunk, (qc, pc))
        o = jnp.moveaxis(o, 0, 1).reshape(B, S, N_HEADS, HEAD_DIM)
        new = (_last_rows(k, wb), _last_rows(v, wb))
    else:
        pk, pv = past
        P = pk.shape[1]
        kk = jnp.concatenate([pk, k], axis=1)
        vv = jnp.concatenate([pv, v], axis=1)
        kpos = PAST_LEN - P + jnp.arange(P + S, dtype=jnp.int32)
        o = _band_block(q, pos, kk, vv, kpos, rel_bias)
        new = (kk[:, S:], vv[:, S:])
    return o.reshape(B, S, D_MODEL) @ w_out, new


def _stick_block(q, qpos, k, v, kpos):
    z = jnp.einsum("bqhd,bkhd->bhqk", q, k).astype(jnp.float32)
    causal = (kpos[None, :] < qpos[:, None])[None, None]
    lsz = jax.nn.log_sigmoid(z)
    log_keep = jnp.where(causal, lsz - z, 0.0)
    after = _rev_cumsum(log_keep) - log_keep
    a = jnp.where(causal, jnp.exp(lsz + after), 0.0)
    return jnp.einsum("bhqk,bkhd->bqhd", a.astype(v.dtype), v)


def _mixer_c(h, pos, past, w_in, w_out):
    B, S, _ = h.shape
    q, k, v = jnp.split(h @ w_in, 3, axis=-1)
    q = q.reshape(B, S, N_HEADS, HEAD_DIM) * SCALE
    k = k.reshape(B, S, N_HEADS, HEAD_DIM)
    v = v.reshape(B, S, N_HEADS, HEAD_DIM)
    if past is None:
        o = _causal_sweep(lambda qs, qp, ks, kp: _stick_block(qs[0], qp, ks[0], ks[1], kp),
                          (q,), (k, v), pos)
    else:
        pk, pv = past
        kk = jnp.concatenate([pk, k], axis=1)
        vv = jnp.concatenate([pv, v], axis=1)
        o = _stick_block(q, pos, kk, vv, jnp.arange(kk.shape[1], dtype=jnp.int32))
    return o.reshape(B, S, D_MODEL) @ w_out, (k, v)


def _fox_block(q, qpos, cq, k, v, kpos, ck):
    s = jnp.einsum("bqhd,bkhd->bhqk", q, k).astype(jnp.float32)
    s = s + jnp.transpose(cq, (0, 2, 1))[:, :, :, None] - jnp.transpose(ck, (0, 2, 1))[:, :, None, :]
    mask = (kpos[None, :] <= qpos[:, None])[None, None]
    s = jnp.where(mask, s, NEG)
    p = jnp.exp(s - jnp.max(s, axis=-1, keepdims=True))
    l = jnp.transpose(jnp.sum(p, axis=-1), (0, 2, 1))
    o = jnp.einsum("bhqk,bkhd->bqhd", p.astype(v.dtype), v).astype(jnp.float32)
    return (o / l[..., None]).astype(v.dtype)


def _mixer_d(h, pos, past, w_in, b_f, w_out):
    B, S, _ = h.shape
    proj = h @ w_in
    q, k, v = jnp.split(proj[..., :3 * D_MODEL], 3, axis=-1)
    q = q.reshape(B, S, N_HEADS, HEAD_DIM) * SCALE
    k = k.reshape(B, S, N_HEADS, HEAD_DIM)
    v = v.reshape(B, S, N_HEADS, HEAD_DIM)
    logf = jax.nn.log_sigmoid(proj[..., 3 * D_MODEL:].astype(jnp.float32) + b_f.astype(jnp.float32))
    if past is None:
        c = jnp.cumsum(logf, axis=1)
        o = _causal_sweep(lambda qs, qp, ks, kp: _fox_block(qs[0], qp, qs[1], ks[0], ks[1], kp, ks[2]),
                          (q, c), (k, v, c), pos)
    else:
        pk, pv, plf = past
        P = pk.shape[1]
        kk = jnp.concatenate([pk, k], axis=1)
        vv = jnp.concatenate([pv, v], axis=1)
        c_all = jnp.cumsum(jnp.concatenate([plf.astype(jnp.float32), logf], axis=1), axis=1)
        o = _fox_block(q, pos, c_all[:, P:], kk, vv, jnp.arange(P + S, dtype=jnp.int32), c_all)
    return o.reshape(B, S, D_MODEL) @ w_out, (k, v, logf)


def _stack(states, j):
    return jnp.stack([s[j] for s in states], axis=0)


def setup_inputs(seed: int = 0) -> dict:
    key = jax.random.key(seed)
    ks = iter(jax.random.split(key, 40))
    L = LAYERS_PER_MIXER
    wb = min(BAND_PAST, PAST_LEN)

    def nrm(shape, scale):
        return scale * jax.random.normal(next(ks), shape, jnp.float32)

    kv_shape = (L, DEC_BATCH, PAST_LEN, N_HEADS, HEAD_DIM)
    band_shape = (L, DEC_BATCH, wb, N_HEADS, HEAD_DIM)
    din = D_MODEL ** -0.5
    return {
        "x_prompt": nrm((BATCH, SEQ, D_MODEL), 1.0),
        "x_sample": nrm((DEC_BATCH, DEC_SEQ, D_MODEL), 1.0),
        "cache_a_k": nrm((L, DEC_BATCH, PAST_LEN, 2 * DIFF_HEADS, HEAD_DIM), 1.0),
        "cache_a_v": nrm((L, DEC_BATCH, PAST_LEN, DIFF_HEADS, DIFF_VDIM), 1.0),
        "cache_b_k": nrm(band_shape, 1.0),
        "cache_b_v": nrm(band_shape, 1.0),
        "cache_c_k": nrm(kv_shape, 1.0),
        "cache_c_v": nrm(kv_shape, 1.0),
        "cache_d_k": nrm(kv_shape, 1.0),
        "cache_d_v": nrm(kv_shape, 1.0),
        "cache_d_logf": jax.nn.log_sigmoid(FORGET_BIAS_INIT + nrm((L, DEC_BATCH, PAST_LEN, N_HEADS), 1.0)),
        "norm_g": 1.0 + nrm((DEPTH, 4, D_MODEL), 0.05),
        "w_up": nrm((DEPTH, D_MODEL, D_FF), din),
        "w_down": nrm((DEPTH, D_FF, D_MODEL), D_FF ** -0.5),
        "a_w_in": nrm((L, D_MODEL, 3 * D_MODEL), din),
        "a_lam_q1": nrm((L, HEAD_DIM), 0.1),
        "a_lam_k1": nrm((L, HEAD_DIM), 0.1),
        "a_lam_q2": nrm((L, HEAD_DIM), 0.1),
        "a_lam_k2": nrm((L, HEAD_DIM), 0.1),
        "a_sub_g": 1.0 + nrm((L, DIFF_VDIM), 0.05),
        "a_w_out": nrm((L, D_MODEL, D_MODEL), din),
        "b_w_in": nrm((L, D_MODEL, 3 * D_MODEL), din),
        "b_rel_bias": nrm((L, N_HEADS, 2 * REL_CLIP + 1), 0.1),
        "b_w_out": nrm((L, D_MODEL, D_MODEL), din),
        "c_w_in": nrm((L, D_MODEL, 3 * D_MODEL), din),
        "c_w_out": nrm((L, D_MODEL, D_MODEL), din),
        "d_w_in": nrm((L, D_MODEL, 3 * D_MODEL + N_HEADS), din),
        "d_b_f": FORGET_BIAS_INIT + nrm((L, N_HEADS), 0.1),
        "d_w_out": nrm((L, D_MODEL, D_MODEL), din),
    }


def reference(x_prompt, x_sample, cache_a_k, cache_a_v, cache_b_k, cache_b_v,
              cache_c_k, cache_c_v, cache_d_k, cache_d_v, cache_d_logf,
              norm_g, w_up, w_down,
              a_w_in, a_lam_q1, a_lam_k1, a_lam_q2, a_lam_k2, a_sub_g, a_w_out,
              b_w_in, b_rel_bias, b_w_out,
              c_w_in, c_w_out,
              d_w_in, d_b_f, d_w_out):
    pos_p = jnp.arange(x_prompt.shape[1], dtype=jnp.int32)
    pos_s = PAST_LEN + jnp.arange(x_sample.shape[1], dtype=jnp.int32)
    st = [([], []) for _ in range(N_MIXERS)]
    xp, xs = x_prompt, x_sample
    for i in range(DEPTH):
        m, r = i % N_MIXERS, i // N_MIXERS
        hp, hs = _rmsnorm(xp, norm_g[i, 0]), _rmsnorm(xs, norm_g[i, 0])
        if m == 0:
            lam_init = 0.8 - 0.6 * math.exp(-0.3 * i)
            lam = (jnp.exp(jnp.sum(a_lam_q1[r].astype(jnp.float32) * a_lam_k1[r].astype(jnp.float32)))
                   - jnp.exp(jnp.sum(a_lam_q2[r].astype(jnp.float32) * a_lam_k2[r].astype(jnp.float32)))
                   + lam_init)
            wts = (a_w_in[r], lam, a_sub_g[r], a_w_out[r], lam_init)
            op, sp = _mixer_a(hp, pos_p, None, *wts)
            os_, ss = _mixer_a(hs, pos_s, (cache_a_k[r], cache_a_v[r]), *wts)
        elif m == 1:
            wts = (b_w_in[r], b_rel_bias[r], b_w_out[r])
            op, sp = _mixer_b(hp, pos_p, None, *wts)
            os_, ss = _mixer_b(hs, pos_s, (cache_b_k[r], cache_b_v[r]), *wts)
        elif m == 2:
            wts = (c_w_in[r], c_w_out[r])
            op, sp = _mixer_c(hp, pos_p, None, *wts)
            os_, ss = _mixer_c(hs, pos_s, (cache_c_k[r], cache_c_v[r]), *wts)
        else:
            wts = (d_w_in[r], d_b_f[r], d_w_out[r])
            op, sp = _mixer_d(hp, pos_p, None, *wts)
            os_, ss = _mixer_d(hs, pos_s, (cache_d_k[r], cache_d_v[r], cache_d_logf[r]), *wts)
        st[m][0].append(sp)
        st[m][1].append(ss)
        xp = xp + _rmsnorm(op, norm_g[i, 1])
        xs = xs + _rmsnorm(os_, norm_g[i, 1])
        xp = xp + _rmsnorm(_mlp(_rmsnorm(xp, norm_g[i, 2]), w_up[i], w_down[i]), norm_g[i, 3])
        xs = xs + _rmsnorm(_mlp(_rmsnorm(xs, norm_g[i, 2]), w_up[i], w_down[i]), norm_g[i, 3])
    pa, sa = st[0]
    pb, sb = st[1]
    pc, sc = st[2]
    pd, sd = st[3]
    return (xp, xs,
            _stack(pa, 0), _stack(pa, 1), _stack(sa, 0), _stack(sa, 1),
            _stack(pb, 0), _stack(pb, 1), _stack(sb, 0), _stack(sb, 1),
            _stack(pc, 0), _stack(pc, 1), _stack(sc, 0), _stack(sc, 1),
            _stack(pd, 0), _stack(pd, 1), _stack(pd, 2), _stack(sd, 0), _stack(sd, 1), _stack(sd, 2))
```

```python
import functools
import math

import jax
import jax.numpy as jnp
from jax import lax
from jax.experimental import pallas as pl
from jax.experimental.pallas import tpu as pltpu

D_MODEL = 1024
HEAD_DIM = 128
N_HEADS = D_MODEL // HEAD_DIM
DIFF_HEADS = N_HEADS // 2
DIFF_VDIM = 2 * HEAD_DIM
N_MIXERS = 4
CHUNK = 64
ROPE_THETA = 500000.0
ROPE_DIM = HEAD_DIM // 4
ROPE_HALF = ROPE_DIM // 2
BAND_CHUNKS = 8
BAND_PAST = BAND_CHUNKS * CHUNK
REL_CLIP = 128
D_FF = 4 * D_MODEL
NORM_EPS = 1e-6
SCALE = HEAD_DIM ** -0.5
NEG = -1e30

LANES = 128
TOKEN_TILE = 512
Q_TILE = 512
STICK_K_TILE = 256
STICK_UNDERFLOW = -104.0
VMEM_LIMIT_BYTES = 56 * 1024 * 1024

BF16 = jnp.bfloat16
F32 = jnp.float32


def _dot(a, b):
    return jnp.dot(a, b, preferred_element_type=F32)


def _dot_nt(a, b):
    return lax.dot_general(a, b, (((1,), (1,)), ((), ())), preferred_element_type=F32)


def _rms(xf, g):
    ms = jnp.mean(xf * xf, axis=-1, keepdims=True)
    return xf * lax.rsqrt(ms + NORM_EPS) * g


def _split3(x):
    hi = x.astype(BF16)
    r = x - hi.astype(F32)
    mid = r.astype(BF16)
    lo = (r - mid.astype(F32)).astype(BF16)
    return hi, mid, lo


def _dot3(x, t):
    hi, mid, lo = _split3(x)
    return _dot(hi, t) + _dot(mid, t) + _dot(lo, t)


def _dot3_t(t, x):
    hi, mid, lo = _split3(x)
    return _dot(t, hi) + _dot(t, mid) + _dot(t, lo)


def _log_sigmoid_pair(z):
    t = jnp.log1p(jnp.exp(-jnp.abs(z)))
    return jnp.minimum(z, 0.0) - t, -jnp.maximum(z, 0.0) - t


def _params(*sem):
    return pltpu.CompilerParams(dimension_semantics=sem, vmem_limit_bytes=VMEM_LIMIT_BYTES)


def _const_spec(shape):
    nd = len(shape)
    return pl.BlockSpec(shape, lambda *_: (0,) * nd, pipeline_mode=pl.Buffered(1))


def _proj_in_kernel(*refs, rope, scale_q, kv_last, fox, nblk):
    it = iter(refs)
    x_ref, g_ref, w_ref = next(it), next(it), next(it)
    if rope:
        cos_ref, sa_ref, sb_ref = next(it), next(it), next(it)
    if fox:
        wf_ref, bf_ref, tril_ref = next(it), next(it), next(it)
    qb_ref = next(it)
    if kv_last:
        kb_ref, vb_ref, kf_ref, vf_ref = next(it), next(it), next(it), next(it)
    else:
        kf_ref, kb_ref, vf_ref, vb_ref = next(it), next(it), next(it), next(it)
    if fox:
        lf_ref, c_ref, carry_ref = next(it), next(it), next(it)

    i = pl.program_id(0)
    h = _rms(x_ref[...], g_ref[...]).astype(BF16)
    cw = 4 * HEAD_DIM
    is_last = (i % nblk) == (nblk - 1)

    def rotate(xh):
        return (xh * cos_ref[...] + pltpu.roll(xh, HEAD_DIM - ROPE_HALF, 1) * sa_ref[...]
                + pltpu.roll(xh, ROPE_HALF, 1) * sb_ref[...])

    for part in range(3):
        for c in range(D_MODEL // cw):
            col = c * cw
            acc = _dot(h, w_ref[:, part * D_MODEL + col:part * D_MODEL + col + cw])
            if rope and part < 2:
                acc = jnp.concatenate(
                    [rotate(acc[:, hh * HEAD_DIM:(hh + 1) * HEAD_DIM]) for hh in range(cw // HEAD_DIM)], axis=1)
            if part == 0:
                if scale_q:
                    acc = acc * SCALE
                qb_ref[:, col:col + cw] = acc.astype(BF16)
            else:
                fref, bref = (kf_ref, kb_ref) if part == 1 else (vf_ref, vb_ref)
                bref[:, col:col + cw] = acc.astype(BF16)
                if kv_last:
                    @pl.when(is_last)
                    def _():
                        fref[:, col:col + cw] = acc
                else:
                    fref[:, col:col + cw] = acc

    if fox:
        lf, _ = _log_sigmoid_pair(_dot(h, wf_ref[...]) + bf_ref[...])

        @pl.when((i % nblk) == 0)
        def _():
            carry_ref[...] = jnp.zeros_like(carry_ref)

        c_tile = _dot3_t(tril_ref[...], lf) + carry_ref[0:1, :]
        lf_ref[...] = lf
        c_ref[...] = c_tile
        carry_ref[...] = jnp.broadcast_to(c_tile[-1:, :], carry_ref.shape)


def _proj_in(x, g, w, *, nblk, rope=None, scale_q=False, kv_last=False, fox=None):
    T = x.shape[0]
    tm = TOKEN_TILE
    assert T % tm == 0 and (T // tm) % nblk == 0
    n = T // tm
    row = lambda i: (i, 0)
    in_specs = [pl.BlockSpec((tm, D_MODEL), row), _const_spec((1, D_MODEL)), _const_spec((D_MODEL, 3 * D_MODEL))]
    args = [x, g.reshape(1, D_MODEL), w]
    if rope is not None:
        tab = lambda i: (i % nblk, 0)
        in_specs += [pl.BlockSpec((tm, HEAD_DIM), tab)] * 3
        args += list(rope)
    if fox is not None:
        in_specs += [_const_spec((D_MODEL, LANES)), _const_spec((1, LANES)), _const_spec((tm, tm))]
        args += list(fox)
    full_b = jax.ShapeDtypeStruct((T, D_MODEL), BF16)
    full_f = jax.ShapeDtypeStruct((T, D_MODEL), F32)
    tile = pl.BlockSpec((tm, D_MODEL), row)
    if kv_last:
        last_f = jax.ShapeDtypeStruct((T // nblk, D_MODEL), F32)
        last = pl.BlockSpec((tm, D_MODEL), lambda i: (i // nblk, 0))
        out_shape = [full_b, full_b, full_b, last_f, last_f]
        out_specs = [tile, tile, tile, last, last]
    else:
        out_shape = [full_b, full_f, full_b, full_f, full_b]
        out_specs = [tile] * 5
    scratch = []
    if fox is not None:
        out_shape += [jax.ShapeDtypeStruct((T, LANES), F32)] * 2
        out_specs += [pl.BlockSpec((tm, LANES), row)] * 2
        scratch = [pltpu.VMEM((8, LANES), F32)]
    kern = functools.partial(_proj_in_kernel, rope=rope is not None, scale_q=scale_q, kv_last=kv_last,
                             fox=fox is not None, nblk=nblk)
    return pl.pallas_call(
        kern, grid=(n,), in_specs=in_specs, out_specs=out_specs, out_shape=out_shape,
        scratch_shapes=scratch, compiler_params=_params("arbitrary"), name="proj_in")(*args)


def _out_mlp_kernel(o_ref, x_ref, g_ref, wo_ref, wu_ref, wd_ref, y_ref):
    a = _dot(o_ref[...], wo_ref[...])
    x1 = x_ref[...] + _rms(a, g_ref[0:1, :])
    h2 = _rms(x1, g_ref[1:2, :]).astype(BF16)
    acc = jnp.zeros_like(x1)
    for c in range(D_FF // D_MODEL):
        u = jnp.maximum(_dot(h2, wu_ref[:, c * D_MODEL:(c + 1) * D_MODEL]), 0.0)
        acc = acc + _dot((u * u).astype(BF16), wd_ref[c * D_MODEL:(c + 1) * D_MODEL, :])
    y_ref[...] = x1 + _rms(acc, g_ref[2:3, :])


def _out_mlp(o, x, g3, wo, wu, wd):
    T = x.shape[0]
    tm = TOKEN_TILE
    row = lambda i: (i, 0)
    return pl.pallas_call(
        _out_mlp_kernel, grid=(T // tm,),
        in_specs=[pl.BlockSpec((tm, D_MODEL), row), pl.BlockSpec((tm, D_MODEL), row), _const_spec((3, D_MODEL)),
                  _const_spec((D_MODEL, D_MODEL)), _const_spec((D_MODEL, D_FF)), _const_spec((D_FF, D_MODEL))],
        out_specs=pl.BlockSpec((tm, D_MODEL), row),
        out_shape=jax.ShapeDtypeStruct((T, D_MODEL), F32),
        compiler_params=_params("arbitrary"), name="out_mlp")(o, x, g3, wo, wu, wd)


def _online_update(s, vc, m_ref, l_ref, acc_ref):
    m_prev = m_ref[...]
    m_new = jnp.maximum(m_prev, jnp.max(s, axis=-1, keepdims=True))
    alpha = jnp.exp(m_prev - m_new)
    p = jnp.exp(s - m_new)
    l_ref[...] = alpha * l_ref[...] + jnp.sum(p, axis=-1, keepdims=True)
    acc_ref[...] = alpha * acc_ref[...] + _dot(p.astype(BF16), vc)
    m_ref[...] = m_new


def _reset(m_ref, l_ref, acc_ref):
    m_ref[...] = jnp.full_like(m_ref, NEG)
    l_ref[...] = jnp.zeros_like(l_ref)
    acc_ref[...] = jnp.zeros_like(acc_ref)


def _iota2(shape):
    return lax.broadcasted_iota(jnp.int32, shape, 0), lax.broadcasted_iota(jnp.int32, shape, 1)


def _diff_lambda(lam_ref, lam_init):
    lv = lam_ref[...]
    s1 = jnp.sum(lv[0:1, :] * lv[1:2, :], axis=-1, keepdims=True)
    s2 = jnp.sum(lv[2:3, :] * lv[3:4, :], axis=-1, keepdims=True)
    return jnp.exp(s1) - jnp.exp(s2) + lam_init


def _diff_finish(o0, o1, lam, subg, lam_init):
    o = o0 - lam * o1
    return (_rms(o, subg) * (1.0 - lam_init)).astype(BF16)


def _stick_tile(z, vc, tri, r_prev, causal):
    lsz, lk = _log_sigmoid_pair(z)
    if causal is not None:
        lk = jnp.where(causal, lk, 0.0)
    after = _dot3(lk, tri) + r_prev
    a = jnp.exp(lsz + after)
    if causal is not None:
        a = jnp.where(causal, a, 0.0)
    return _dot(a.astype(BF16), vc), r_prev + jnp.sum(lk, axis=-1, keepdims=True)


def _attn_a_prompt_kernel(lam_ref, subg_ref, q_ref, k_ref, v_ref, o_ref, m_ref, l_ref, acc_ref, *, lam_init):
    tq = q_ref.shape[0]
    i = pl.program_id(2)
    q0 = i * tq
    ri, ci = _iota2((tq, tq))
    diag_mask = (ci // CHUNK) <= (ri // CHUNK)
    outs = []
    for mm in range(2):
        cols = slice(mm * HEAD_DIM, (mm + 1) * HEAD_DIM)
        q = q_ref[:, cols]
        _reset(m_ref, l_ref, acc_ref)

        def body(kb, carry, q=q, cols=cols):
            k0 = pl.multiple_of(kb * tq, tq)
            s = _dot_nt(q, k_ref[pl.ds(k0, tq), cols])
            _online_update(s, v_ref[pl.ds(k0, tq), :], m_ref, l_ref, acc_ref)
            return carry

        lax.fori_loop(0, i, body, 0)
        k0 = pl.multiple_of(q0, tq)
        s = jnp.where(diag_mask, _dot_nt(q, k_ref[pl.ds(k0, tq), cols]), NEG)
        _online_update(s, v_ref[pl.ds(k0, tq), :], m_ref, l_ref, acc_ref)
        outs.append(acc_ref[...] / l_ref[...])
    lam = _diff_lambda(lam_ref, lam_init)
    o_ref[...] = _diff_finish(outs[0], outs[1], lam, subg_ref[...], lam_init)


def _attn_a_prompt(q, k, v, lamvec, subg, B, S, lam_init):
    tq = Q_TILE
    nq = S // tq
    w = DIFF_VDIM
    kern = functools.partial(_attn_a_prompt_kernel, lam_init=lam_init)
    return pl.pallas_call(
        kern, grid=(B, DIFF_HEADS, nq),
        in_specs=[_const_spec((4, HEAD_DIM)), _const_spec((1, w)),
                  pl.BlockSpec((tq, w), lambda b, g, i: (b * nq + i, g)),
                  pl.BlockSpec((S, w), lambda b, g, i: (b, g)),
                  pl.BlockSpec((S, w), lambda b, g, i: (b, g))],
        out_specs=pl.BlockSpec((tq, w), lambda b, g, i: (b * nq + i, g)),
        out_shape=jax.ShapeDtypeStruct((B * S, D_MODEL), BF16),
        scratch_shapes=[pltpu.VMEM((tq, 1), F32), pltpu.VMEM((tq, 1), F32), pltpu.VMEM((tq, w), F32)],
        compiler_params=_params("arbitrary", "arbitrary", "arbitrary"), name="attn_a_prompt")(lamvec, subg, q, k, v)


def _attn_b_prompt_kernel(bias_ref, q_ref, k_ref, v_ref, o_ref):
    tq = q_ref.shape[0]
    i = pl.program_id(2)
    q0 = pl.multiple_of(i * tq, tq)
    p0 = pl.multiple_of(jnp.maximum(i - 1, 0) * tq, tq)
    q = q_ref[...]
    s_cur = _dot_nt(q, k_ref[pl.ds(q0, tq), :]) * SCALE + bias_ref[:, tq:]
    s_past = _dot_nt(q, k_ref[pl.ds(p0, tq), :]) * SCALE + bias_ref[:, :tq]
    s_past = jnp.where(i > 0, s_past, NEG)
    m = jnp.maximum(jnp.max(s_cur, axis=-1, keepdims=True), jnp.max(s_past, axis=-1, keepdims=True))
    p_cur = jnp.exp(s_cur - m)
    p_past = jnp.exp(s_past - m)
    l = jnp.sum(p_cur, axis=-1, keepdims=True) + jnp.sum(p_past, axis=-1, keepdims=True)
    o = _dot(p_cur.astype(BF16), v_ref[pl.ds(q0, tq), :]) + _dot(p_past.astype(BF16), v_ref[pl.ds(p0, tq), :])
    o_ref[...] = (o / l).astype(BF16)


def _attn_b_prompt(q, k, v, bias, B, S):
    tq = Q_TILE
    assert tq == BAND_PAST
    nq = S // tq
    d = HEAD_DIM
    return pl.pallas_call(
        _attn_b_prompt_kernel, grid=(B, N_HEADS, nq),
        in_specs=[pl.BlockSpec((None, tq, 2 * tq), lambda b, h, i: (h, 0, 0)),
                  pl.BlockSpec((tq, d), lambda b, h, i: (b * nq + i, h)),
                  pl.BlockSpec((S, d), lambda b, h, i: (b, h)),
                  pl.BlockSpec((S, d), lambda b, h, i: (b, h))],
        out_specs=pl.BlockSpec((tq, d), lambda b, h, i: (b * nq + i, h)),
        out_shape=jax.ShapeDtypeStruct((B * S, D_MODEL), BF16),
        compiler_params=_params("arbitrary", "arbitrary", "arbitrary"), name="attn_b_prompt")(bias, q, k, v)


def _attn_c_prompt_kernel(tri_ref, q_ref, k_ref, v_ref, o_ref, r_ref, acc_ref):
    tq = q_ref.shape[0]
    tk = STICK_K_TILE
    nsub = tq // tk
    i = pl.program_id(2)
    q = q_ref[...]
    tri = tri_ref[...]
    r_ref[...] = jnp.zeros_like(r_ref)
    acc_ref[...] = jnp.zeros_like(acc_ref)
    ri, ci = _iota2((tq, tk))

    def step(k0, causal):
        pv, r_new = _stick_tile(_dot_nt(q, k_ref[pl.ds(k0, tk), :]), v_ref[pl.ds(k0, tk), :], tri,
                                r_ref[...], causal)
        acc_ref[...] += pv
        r_ref[...] = r_new

    for d in reversed(range(nsub)):
        step(pl.multiple_of(i * tq + d * tk, tk), (ci + d * tk) < ri)

    def cond(c):
        kb, rmax = c
        return jnp.logical_and(kb >= 0, rmax > STICK_UNDERFLOW)

    def body(c):
        kb, _ = c
        step(pl.multiple_of(kb * tk, tk), None)
        return kb - 1, jnp.max(r_ref[...])

    lax.while_loop(cond, body, (i * nsub - 1, jnp.max(r_ref[...])))
    o_ref[...] = acc_ref[...].astype(BF16)


def _attn_c_prompt(q, k, v, tri, B, S):
    tq = Q_TILE
    nq = S // tq
    d = HEAD_DIM
    return pl.pallas_call(
        _attn_c_prompt_kernel, grid=(B, N_HEADS, nq),
        in_specs=[_const_spec((STICK_K_TILE, STICK_K_TILE)),
                  pl.BlockSpec((tq, d), lambda b, h, i: (b * nq + i, h)),
                  pl.BlockSpec((S, d), lambda b, h, i: (b, h)),
                  pl.BlockSpec((S, d), lambda b, h, i: (b, h))],
        out_specs=pl.BlockSpec((tq, d), lambda b, h, i: (b * nq + i, h)),
        out_shape=jax.ShapeDtypeStruct((B * S, D_MODEL), BF16),
        scratch_shapes=[pltpu.VMEM((tq, 1), F32), pltpu.VMEM((tq, d), F32)],
        compiler_params=_params("arbitrary", "arbitrary", "arbitrary"), name="attn_c_prompt")(tri, q, k, v)


def _attn_d_prompt_kernel(q_ref, k_ref, v_ref, cq_ref, ck_ref, o_ref, m_ref, l_ref, acc_ref):
    tq = q_ref.shape[0]
    i = pl.program_id(2)
    q = q_ref[...]
    cq = cq_ref[...]
    _reset(m_ref, l_ref, acc_ref)

    def scores(k0):
        return _dot_nt(q, k_ref[pl.ds(k0, tq), :]) + cq - ck_ref[:, pl.ds(k0, tq)]

    def body(kb, carry):
        k0 = pl.multiple_of(kb * tq, tq)
        _online_update(scores(k0), v_ref[pl.ds(k0, tq), :], m_ref, l_ref, acc_ref)
        return carry

    lax.fori_loop(0, i, body, 0)
    k0 = pl.multiple_of(i * tq, tq)
    ri, ci = _iota2((tq, tq))
    _online_update(jnp.where(ci <= ri, scores(k0), NEG), v_ref[pl.ds(k0, tq), :], m_ref, l_ref, acc_ref)
    o_ref[...] = (acc_ref[...] / l_ref[...]).astype(BF16)


def _attn_d_prompt(q, k, v, cq, ck, B, S):
    tq = Q_TILE
    nq = S // tq
    d = HEAD_DIM
    H = N_HEADS
    return pl.pallas_call(
        _attn_d_prompt_kernel, grid=(B, H, nq),
        in_specs=[pl.BlockSpec((tq, d), lambda b, h, i: (b * nq + i, h)),
                  pl.BlockSpec((S, d), lambda b, h, i: (b, h)),
                  pl.BlockSpec((S, d), lambda b, h, i: (b, h)),
                  pl.BlockSpec((None, tq, 1), lambda b, h, i: (b * H + h, i, 0)),
                  pl.BlockSpec((None, 1, S), lambda b, h, i: (b * H + h, 0, 0))],
        out_specs=pl.BlockSpec((tq, d), lambda b, h, i: (b * nq + i, h)),
        out_shape=jax.ShapeDtypeStruct((B * S, D_MODEL), BF16),
        scratch_shapes=[pltpu.VMEM((tq, 1), F32), pltpu.VMEM((tq, 1), F32), pltpu.VMEM((tq, d), F32)],
        compiler_params=_params("arbitrary", "arbitrary", "arbitrary"), name="attn_d_prompt")(q, k, v, cq, ck)


def _softmax_two(s1, s2, v1, v2):
    m = jnp.maximum(jnp.max(s1, axis=-1, keepdims=True), jnp.max(s2, axis=-1, keepdims=True))
    p1 = jnp.exp(s1 - m)
    p2 = jnp.exp(s2 - m)
    l = jnp.sum(p1, axis=-1, keepdims=True) + jnp.sum(p2, axis=-1, keepdims=True)
    return (_dot(p1.astype(BF16), v1) + _dot(p2.astype(BF16), v2)) / l


def _attn_a_sample_kernel(lam_ref, subg_ref, q_ref, kn_ref, vn_ref, ck_ref, cv_ref, o_ref, *, lam_init, past_len):
    sn = q_ref.shape[0]
    P = ck_ref.shape[0]
    r1, c1 = _iota2((sn, P))
    r2, c2 = _iota2((sn, sn))
    mask1 = (c1 // CHUNK) <= ((r1 + past_len) // CHUNK)
    mask2 = ((c2 + P) // CHUNK) <= ((r2 + past_len) // CHUNK)
    v1 = cv_ref[...].astype(BF16)
    v2 = vn_ref[...]
    outs = []
    for mm in range(2):
        cols = slice(mm * HEAD_DIM, (mm + 1) * HEAD_DIM)
        q = q_ref[:, cols]
        s1 = jnp.where(mask1, _dot_nt(q, ck_ref[:, cols].astype(BF16)), NEG)
        s2 = jnp.where(mask2, _dot_nt(q, kn_ref[:, cols]), NEG)
        outs.append(_softmax_two(s1, s2, v1, v2))
    lam = _diff_lambda(lam_ref, lam_init)
    o_ref[...] = _diff_finish(outs[0], outs[1], lam, subg_ref[...], lam_init)


def _attn_b_sample_kernel(b1_ref, b2_ref, q_ref, kn_ref, vn_ref, ck_ref, cv_ref, o_ref):
    q = q_ref[...]
    s1 = _dot_nt(q, ck_ref[...].astype(BF16)) * SCALE + b1_ref[...]
    s2 = _dot_nt(q, kn_ref[...]) * SCALE + b2_ref[...]
    o_ref[...] = _softmax_two(s1, s2, cv_ref[...].astype(BF16), vn_ref[...]).astype(BF16)


def _attn_c_sample_kernel(trin_ref, tri_ref, q_ref, kn_ref, vn_ref, ck_ref, cv_ref, o_ref):
    sn = q_ref.shape[0]
    P = ck_ref.shape[0]
    tk = STICK_K_TILE
    q = q_ref[...]
    ri, ci = _iota2((sn, sn))
    acc, r = _stick_tile(_dot_nt(q, kn_ref[...]), vn_ref[...], trin_ref[...], jnp.zeros((sn, 1), F32), ci < ri)
    tri = tri_ref[...]
    for kb in reversed(range(P // tk)):
        rows = slice(kb * tk, (kb + 1) * tk)
        pv, r = _stick_tile(_dot_nt(q, ck_ref[rows, :].astype(BF16)), cv_ref[rows, :].astype(BF16), tri, r, None)
        acc = acc + pv
    o_ref[...] = acc.astype(BF16)


def _attn_d_sample_kernel(triu_ref, triun_ref, q_ref, kn_ref, vn_ref, ck_ref, cv_ref, lfp_ref, lfn_ref, o_ref):
    sn = q_ref.shape[0]
    q = q_ref[...]
    c_past = _dot3(jnp.broadcast_to(lfp_ref[...], (8, lfp_ref.shape[1])), triu_ref[...])[0:1, :]
    total = c_past[:, -1:]
    c_new = _dot3(jnp.broadcast_to(lfn_ref[...], (8, sn)), triun_ref[...])[0:1, :] + total
    ri, ci = _iota2((sn, sn))
    cq = jnp.sum(jnp.where(ri == ci, jnp.broadcast_to(c_new, (sn, sn)), 0.0), axis=-1, keepdims=True)
    s1 = _dot_nt(q, ck_ref[...].astype(BF16)) + cq - c_past
    s2 = jnp.where(ci <= ri, _dot_nt(q, kn_ref[...]) + cq - c_new, NEG)
    o_ref[...] = _softmax_two(s1, s2, cv_ref[...].astype(BF16), vn_ref[...]).astype(BF16)


def _attn_sample(kern, consts, q, kn, vn, ck, cv, extra, nb, sn, P, nh, name):
    w = D_MODEL // nh
    bh = lambda b, h: (b, h)
    in_specs = [_const_spec(c.shape) for c in consts]
    in_specs += [pl.BlockSpec((sn, w), bh)] * 3 + [pl.BlockSpec((P, w), bh)] * 2
    in_specs += [pl.BlockSpec((None, 1, n), lambda b, h: (b * nh + h, 0, 0)) for _, n in extra]
    return pl.pallas_call(
        kern, grid=(nb, nh), in_specs=in_specs,
        out_specs=pl.BlockSpec((sn, w), bh),
        out_shape=jax.ShapeDtypeStruct((nb * sn, D_MODEL), BF16),
        compiler_params=_params("arbitrary", "arbitrary"), name=name)(
            *consts, q, kn, vn, ck, cv, *[a for a, _ in extra])


def _rope_tables(pos):
    inv = jnp.exp(-math.log(ROPE_THETA) * jnp.arange(ROPE_HALF, dtype=F32) * (2.0 / ROPE_DIM))
    ang = pos.astype(F32)[:, None] * inv[None, :]
    cos, sin = jnp.cos(ang), jnp.sin(ang)
    n = pos.shape[0]
    pad = jnp.zeros((n, HEAD_DIM - ROPE_DIM), F32)
    zero = jnp.zeros((n, ROPE_HALF), F32)
    c = jnp.concatenate([cos, cos, pad + 1.0], axis=1)
    sa = jnp.concatenate([-sin, zero, pad], axis=1)
    sb = jnp.concatenate([zero, sin, pad], axis=1)
    return c, sa, sb


def _band_bias(rel_bias, qpos, kpos):
    rel = jnp.clip(qpos[:, None] - kpos[None, :], -REL_CLIP, REL_CLIP) + REL_CLIP
    qc, kc = qpos[:, None] // CHUNK, kpos[None, :] // CHUNK
    mask = (kc <= qc) & (kc >= qc - BAND_CHUNKS)
    return jnp.where(mask[None], rel_bias.astype(F32)[:, rel], NEG)


def _tri(n, fn):
    i = jnp.arange(n)
    return fn(i[:, None], i[None, :]).astype(BF16)


def _rows_per_head(a, nb, n):
    return jnp.transpose(a[:, :N_HEADS].reshape(nb, n, N_HEADS), (0, 2, 1)).reshape(nb * N_HEADS, 1, n)


def kernel(x_prompt, x_sample, cache_a_k, cache_a_v, cache_b_k, cache_b_v, cache_c_k, cache_c_v, cache_d_k, cache_d_v, cache_d_logf, norm_g, w_up, w_down, a_w_in, a_lam_q1, a_lam_k1, a_lam_q2, a_lam_k2, a_sub_g, a_w_out, b_w_in, b_rel_bias, b_w_out, c_w_in, c_w_out, d_w_in, d_b_f, d_w_out):
    B, S, _ = x_prompt.shape
    NB, SN, _ = x_sample.shape
    P = cache_a_k.shape[2]
    PB = cache_b_k.shape[2]
    depth = norm_g.shape[0]
    H = N_HEADS
    tm = TOKEN_TILE
    nblk = S // tm
    assert NB * SN == tm and S % Q_TILE == 0 and PB == BAND_PAST

    xp = x_prompt.reshape(B * S, D_MODEL)
    xs = x_sample.reshape(NB * SN, D_MODEL)
    pos_p = jnp.arange(S, dtype=jnp.int32)
    pos_s = P + jnp.arange(SN, dtype=jnp.int32)

    flat = lambda c: c.reshape(c.shape[0] * c.shape[1], D_MODEL)
    tri_strict = _tri(STICK_K_TILE, lambda r, c: r > c)
    st = [([], []) for _ in range(N_MIXERS)]

    for i in range(depth):
        m, r = i % N_MIXERS, i // N_MIXERS
        g = norm_g[i].astype(F32)
        if m == 0:
            lam_init = 0.8 - 0.6 * math.exp(-0.3 * i)
            w = a_w_in[r].astype(BF16)
            lamvec = jnp.stack([a_lam_q1[r], a_lam_k1[r], a_lam_q2[r], a_lam_k2[r]]).astype(F32)
            subg = a_sub_g[r].astype(F32).reshape(1, DIFF_VDIM)
            qb, kf, kb, vf, vb = _proj_in(xp, g[0], w, nblk=nblk, rope=_rope_tables(pos_p), scale_q=True)
            op = _attn_a_prompt(qb, kb, vb, lamvec, subg, B, S, lam_init)
            sp = (kf.reshape(B, S, H, HEAD_DIM), vf.reshape(B, S, DIFF_HEADS, DIFF_VDIM))
            rope_s = tuple(jnp.tile(t, (NB, 1)) for t in _rope_tables(pos_s))
            qb, kf, kb, vf, vb = _proj_in(xs, g[0], w, nblk=1, rope=rope_s, scale_q=True)
            kern = functools.partial(_attn_a_sample_kernel, lam_init=lam_init, past_len=P)
            os_ = _attn_sample(kern, [lamvec, subg], qb, kb, vb, flat(cache_a_k[r]), flat(cache_a_v[r]), [],
                               NB, SN, P, DIFF_HEADS, "attn_a_sample")
            ss = (kf.reshape(NB, SN, H, HEAD_DIM), vf.reshape(NB, SN, DIFF_HEADS, DIFF_VDIM))
            w_out = a_w_out[r]
        elif m == 1:
            w = b_w_in[r].astype(BF16)
            qb, kb, vb, kl, vl = _proj_in(xp, g[0], w, nblk=nblk, kv_last=True)
            tq = Q_TILE
            rel_q = jnp.arange(tq, dtype=jnp.int32) + tq
            bias = _band_bias(b_rel_bias[r], rel_q, jnp.arange(2 * tq, dtype=jnp.int32))
            op = _attn_b_prompt(qb, kb, vb, bias, B, S)
            sp = (kl.reshape(B, PB, H, HEAD_DIM), vl.reshape(B, PB, H, HEAD_DIM))
            qb, kb, vb, kl, vl = _proj_in(xs, g[0], w, nblk=1, kv_last=True)
            kpos = P - PB + jnp.arange(PB + SN, dtype=jnp.int32)
            kvis = _band_bias(b_rel_bias[r], pos_s, kpos)
            kvis = jnp.where((kpos >= 0)[None, None, :], kvis, NEG)
            b1 = jnp.tile(kvis[:, :, :PB], (1, 1, 1)).reshape(H * SN, PB)
            b2 = kvis[:, :, PB:].reshape(H * SN, SN)
            os_ = _attn_b_sample(qb, kb, vb, flat(cache_b_k[r]), flat(cache_b_v[r]), b1, b2, NB, SN, PB)
            kn = kl.reshape(NB, SN, H, HEAD_DIM)
            vn = vl.reshape(NB, SN, H, HEAD_DIM)
            ss = (jnp.concatenate([cache_b_k[r], kn], axis=1)[:, SN:], jnp.concatenate([cache_b_v[r], vn], axis=1)[:, SN:])
            w_out = b_w_out[r]
        elif m == 2:
            w = c_w_in[r].astype(BF16)
            qb, kf, kb, vf, vb = _proj_in(xp, g[0], w, nblk=nblk, scale_q=True)
            op = _attn_c_prompt(qb, kb, vb, tri_strict, B, S)
            sp = (kf.reshape(B, S, H, HEAD_DIM), vf.reshape(B, S, H, HEAD_DIM))
            qb, kf, kb, vf, vb = _proj_in(xs, g[0], w, nblk=1, scale_q=True)
            os_ = _attn_sample(_attn_c_sample_kernel, [_tri(SN, lambda r_, c_: r_ > c_), tri_strict], qb, kb, vb,
                               flat(cache_c_k[r]), flat(cache_c_v[r]), [], NB, SN, P, H, "attn_c_sample")
            ss = (kf.reshape(NB, SN, H, HEAD_DIM), vf.reshape(NB, SN, H, HEAD_DIM))
            w_out = c_w_out[r]
        else:
            w_all = d_w_in[r]
            w = w_all[:, :3 * D_MODEL].astype(BF16)
            wf = jnp.pad(w_all[:, 3 * D_MODEL:], ((0, 0), (0, LANES - H))).astype(BF16)
            bf = jnp.pad(d_b_f[r].astype(F32), (0, LANES - H)).reshape(1, LANES)
            fox = (wf, bf, _tri(tm, lambda r_, c_: r_ >= c_))
            qb, kf, kb, vf, vb, lf, c = _proj_in(xp, g[0], w, nblk=nblk, scale_q=True, fox=fox)
            ck = _rows_per_head(c, B, S)
            op = _attn_d_prompt(qb, kb, vb, ck.reshape(B * H, S, 1), ck, B, S)
            sp = (kf.reshape(B, S, H, HEAD_DIM), vf.reshape(B, S, H, HEAD_DIM), lf[:, :H].reshape(B, S, H))
            qb, kf, kb, vf, vb, lf, c = _proj_in(xs, g[0], w, nblk=1, scale_q=True, fox=fox)
            lfp = jnp.transpose(cache_d_logf[r].astype(F32), (0, 2, 1)).reshape(NB * H, 1, P)
            lfn = _rows_per_head(lf, NB, SN)
            triu = [_tri(P, lambda r_, c_: r_ <= c_), _tri(SN, lambda r_, c_: r_ <= c_)]
            os_ = _attn_sample(_attn_d_sample_kernel, triu, qb, kb, vb, flat(cache_d_k[r]), flat(cache_d_v[r]),
                               [(lfp, P), (lfn, SN)], NB, SN, P, H, "attn_d_sample")
            ss = (kf.reshape(NB, SN, H, HEAD_DIM), vf.reshape(NB, SN, H, HEAD_DIM), lf[:, :H].reshape(NB, SN, H))
            w_out = d_w_out[r]
        st[m][0].append(sp)
        st[m][1].append(ss)
        wo, wu, wd = w_out.astype(BF16), w_up[i].astype(BF16), w_down[i].astype(BF16)
        xp = _out_mlp(op, xp, g[1:4], wo, wu, wd)
        xs = _out_mlp(os_, xs, g[1:4], wo, wu, wd)

    stack = lambda states, j: jnp.stack([s[j] for s in states], axis=0)
    pa, sa = st[0]
    pb, sb = st[1]
    pc, sc = st[2]
    pd, sd = st[3]
    return (xp.reshape(B, S, D_MODEL), xs.reshape(NB, SN, D_MODEL),
            stack(pa, 0), stack(pa, 1), stack(sa, 0), stack(sa, 1),
            stack(pb, 0), stack(pb, 1), stack(sb, 0), stack(sb, 1),
            stack(pc, 0), stack(pc, 1), stack(sc, 0), stack(sc, 1),
            stack(pd, 0), stack(pd, 1), stack(pd, 2), stack(sd, 0), stack(sd, 1), stack(sd, 2))


def _attn_b_sample(q, kn, vn, ck, cv, b1, b2, nb, sn, pb):
    d = HEAD_DIM
    bh = lambda b, h: (b, h)
    return pl.pallas_call(
        _attn_b_sample_kernel, grid=(nb, N_HEADS),
        in_specs=[pl.BlockSpec((sn, pb), lambda b, h: (h, 0)), pl.BlockSpec((sn, sn), lambda b, h: (h, 0)),
                  pl.BlockSpec((sn, d), bh), pl.BlockSpec((sn, d), bh), pl.BlockSpec((sn, d), bh),
                  pl.BlockSpec((pb, d), bh), pl.BlockSpec((pb, d), bh)],
        out_specs=pl.BlockSpec((sn, d), bh),
        out_shape=jax.ShapeDtypeStruct((nb * sn, D_MODEL), BF16),
        compiler_params=_params("arbitrary", "arbitrary"), name="attn_b_sample")(b1, b2, q, kn, vn, ck, cv)
```

```python
import functools
import math

import jax
import jax.numpy as jnp
from jax import lax
from jax.experimental import pallas as pl
from jax.experimental.pallas import tpu as pltpu

D_MODEL = 1024
HEAD_DIM = 128
N_HEADS = D_MODEL // HEAD_DIM
DIFF_HEADS = N_HEADS // 2
DIFF_VDIM = 2 * HEAD_DIM
N_MIXERS = 4
CHUNK = 64
ROPE_THETA = 500000.0
ROPE_DIM = HEAD_DIM // 4
ROPE_HALF = ROPE_DIM // 2
BAND_CHUNKS = 8
BAND_PAST = BAND_CHUNKS * CHUNK
REL_CLIP = 128
D_FF = 4 * D_MODEL
NORM_EPS = 1e-6
SCALE = HEAD_DIM ** -0.5
LOG2E = math.log2(math.e)
SCALE_LOG2E = SCALE * LOG2E
NEG = -1e30

LANES = 128
TOKEN_TILE = 512
Q_TILE = 512
FLASH_ROWS = 128
FLASH_HEADS = 2
STICK_K_TILE = 256
STICK_UNDERFLOW = -104.0
VMEM_LIMIT_BYTES = 56 * 1024 * 1024

BF16 = jnp.bfloat16
F32 = jnp.float32


def _dot(a, b):
    return jnp.dot(a, b, preferred_element_type=F32)


def _dot_nt(a, b):
    return lax.dot_general(a, b, (((1,), (1,)), ((), ())), preferred_element_type=F32)


def _rms(xf, g):
    ms = jnp.mean(xf * xf, axis=-1, keepdims=True)
    return xf * lax.rsqrt(ms + NORM_EPS) * g


def _split3(x):
    hi = x.astype(BF16)
    r = x - hi.astype(F32)
    mid = r.astype(BF16)
    lo = (r - mid.astype(F32)).astype(BF16)
    return hi, mid, lo


def _dot3(x, t):
    hi, mid, lo = _split3(x)
    return _dot(hi, t) + _dot(mid, t) + _dot(lo, t)


def _dot3_t(t, x):
    hi, mid, lo = _split3(x)
    return _dot(t, hi) + _dot(t, mid) + _dot(t, lo)


def _log_sigmoid_pair(z):
    t = jnp.log1p(jnp.exp(-jnp.abs(z)))
    return jnp.minimum(z, 0.0) - t, -jnp.maximum(z, 0.0) - t


def _params(*sem):
    return pltpu.CompilerParams(dimension_semantics=sem, vmem_limit_bytes=VMEM_LIMIT_BYTES)


def _const_spec(shape):
    nd = len(shape)
    return pl.BlockSpec(shape, lambda *_: (0,) * nd, pipeline_mode=pl.Buffered(1))


def _proj_in_kernel(*refs, rope, scale_q, kv_last, fox, nblk):
    it = iter(refs)
    x_ref, g_ref, w_ref = next(it), next(it), next(it)
    if rope:
        cos_ref, sa_ref, sb_ref = next(it), next(it), next(it)
    if fox:
        wf_ref, bf_ref, tril_ref = next(it), next(it), next(it)
    qb_ref = next(it)
    if kv_last:
        kb_ref, vb_ref, kf_ref, vf_ref = next(it), next(it), next(it), next(it)
    else:
        kf_ref, kb_ref, vf_ref, vb_ref = next(it), next(it), next(it), next(it)
    if fox:
        lf_ref, c_ref, carry_ref = next(it), next(it), next(it)

    i = pl.program_id(0)
    h = _rms(x_ref[...], g_ref[...]).astype(BF16)
    cw = 4 * HEAD_DIM
    is_last = (i % nblk) == (nblk - 1)

    def rotate(xh):
        return (xh * cos_ref[...] + pltpu.roll(xh, HEAD_DIM - ROPE_HALF, 1) * sa_ref[...]
                + pltpu.roll(xh, ROPE_HALF, 1) * sb_ref[...])

    for part in range(3):
        for c in range(D_MODEL // cw):
            col = c * cw
            acc = _dot(h, w_ref[:, part * D_MODEL + col:part * D_MODEL + col + cw])
            if rope and part < 2:
                acc = jnp.concatenate(
                    [rotate(acc[:, hh * HEAD_DIM:(hh + 1) * HEAD_DIM]) for hh in range(cw // HEAD_DIM)], axis=1)
            if part == 0:
                if scale_q is not None:
                    acc = acc * scale_q
                qb_ref[:, col:col + cw] = acc.astype(BF16)
            else:
                fref, bref = (kf_ref, kb_ref) if part == 1 else (vf_ref, vb_ref)
                bref[:, col:col + cw] = acc.astype(BF16)
                if kv_last:
                    @pl.when(is_last)
                    def _():
                        fref[:, col:col + cw] = acc
                else:
                    fref[:, col:col + cw] = acc

    if fox:
        lf, _ = _log_sigmoid_pair(_dot(h, wf_ref[...]) + bf_ref[...])

        @pl.when((i % nblk) == 0)
        def _():
            carry_ref[...] = jnp.zeros_like(carry_ref)

        c_tile = _dot3_t(tril_ref[...], lf) + carry_ref[0:1, :]
        lf_ref[...] = lf
        c_ref[...] = c_tile
        carry_ref[...] = jnp.broadcast_to(c_tile[-1:, :], carry_ref.shape)


def _proj_in(x, g, w, *, nblk, rope=None, scale_q=None, kv_last=False, fox=None):
    T = x.shape[0]
    tm = TOKEN_TILE
    assert T % tm == 0 and (T // tm) % nblk == 0
    n = T // tm
    row = lambda i: (i, 0)
    in_specs = [pl.BlockSpec((tm, D_MODEL), row), _const_spec((1, D_MODEL)), _const_spec((D_MODEL, 3 * D_MODEL))]
    args = [x, g.reshape(1, D_MODEL), w]
    if rope is not None:
        tab = lambda i: (i % nblk, 0)
        in_specs += [pl.BlockSpec((tm, HEAD_DIM), tab)] * 3
        args += list(rope)
    if fox is not None:
        in_specs += [_const_spec((D_MODEL, LANES)), _const_spec((1, LANES)), _const_spec((tm, tm))]
        args += list(fox)
    full_b = jax.ShapeDtypeStruct((T, D_MODEL), BF16)
    full_f = jax.ShapeDtypeStruct((T, D_MODEL), F32)
    tile = pl.BlockSpec((tm, D_MODEL), row)
    if kv_last:
        last_f = jax.ShapeDtypeStruct((T // nblk, D_MODEL), F32)
        last = pl.BlockSpec((tm, D_MODEL), lambda i: (i // nblk, 0))
        out_shape = [full_b, full_b, full_b, last_f, last_f]
        out_specs = [tile, tile, tile, last, last]
    else:
        out_shape = [full_b, full_f, full_b, full_f, full_b]
        out_specs = [tile] * 5
    scratch = []
    if fox is not None:
        out_shape += [jax.ShapeDtypeStruct((T, LANES), F32)] * 2
        out_specs += [pl.BlockSpec((tm, LANES), row)] * 2
        scratch = [pltpu.VMEM((8, LANES), F32)]
    kern = functools.partial(_proj_in_kernel, rope=rope is not None, scale_q=scale_q, kv_last=kv_last,
                             fox=fox is not None, nblk=nblk)
    return pl.pallas_call(
        kern, grid=(n,), in_specs=in_specs, out_specs=out_specs, out_shape=out_shape,
        scratch_shapes=scratch, compiler_params=_params("arbitrary"), name="proj_in")(*args)


def _out_mlp_kernel(o_ref, x_ref, g_ref, wo_ref, wu_ref, wd_ref, y_ref):
    a = _dot(o_ref[...], wo_ref[...])
    x1 = x_ref[...] + _rms(a, g_ref[0:1, :])
    h2 = _rms(x1, g_ref[1:2, :]).astype(BF16)
    acc = jnp.zeros_like(x1)
    for c in range(D_FF // D_MODEL):
        u = jnp.maximum(_dot(h2, wu_ref[:, c * D_MODEL:(c + 1) * D_MODEL]), 0.0)
        acc = acc + _dot((u * u).astype(BF16), wd_ref[c * D_MODEL:(c + 1) * D_MODEL, :])
    y_ref[...] = x1 + _rms(acc, g_ref[2:3, :])


def _out_mlp(o, x, g3, wo, wu, wd):
    T = x.shape[0]
    tm = TOKEN_TILE
    row = lambda i: (i, 0)
    return pl.pallas_call(
        _out_mlp_kernel, grid=(T // tm,),
        in_specs=[pl.BlockSpec((tm, D_MODEL), row), pl.BlockSpec((tm, D_MODEL), row), _const_spec((3, D_MODEL)),
                  _const_spec((D_MODEL, D_MODEL)), _const_spec((D_MODEL, D_FF)), _const_spec((D_FF, D_MODEL))],
        out_specs=pl.BlockSpec((tm, D_MODEL), row),
        out_shape=jax.ShapeDtypeStruct((T, D_MODEL), F32),
        compiler_params=_params("arbitrary"), name="out_mlp")(o, x, g3, wo, wu, wd)


def _lane_rep(x, width):
    return x if width == LANES else jnp.concatenate([x] * (width // LANES), axis=1)


def _iota2(shape):
    return lax.broadcasted_iota(jnp.int32, shape, 0), lax.broadcasted_iota(jnp.int32, shape, 1)


def _causal_flash(n_full, qk_fns, v_fns, diag_mask_fn, shift_ref, s_ref, p_ref, alpha_ref, m_ref, l_ref, acc_ref):
    _, n, tq, tk = s_ref.shape
    rb = FLASH_ROWS
    width = acc_ref.shape[2]

    def weigh_values(slot, t):
        for x in range(n):
            acc_ref[x] = _lane_rep(alpha_ref[slot, x], width) * acc_ref[x] + _dot(p_ref[slot, x], v_fns[x](t))

    def softmax_tile(slot, diagonal):
        for x in range(n):
            for r in range(tq // rb):
                rows = slice(r * rb, (r + 1) * rb)
                s = s_ref[slot, x, rows, :]
                if diagonal:
                    s = jnp.where(diag_mask_fn(r * rb, (rb, tk)), s, NEG)
                m_prev = m_ref[x, rows, :]
                m_cur = jnp.broadcast_to(jnp.max(s, axis=-1, keepdims=True), (rb, LANES))
                if shift_ref is not None:
                    m_cur = m_cur + shift_ref[x, rows, :]
                m_new = jnp.maximum(m_prev, m_cur)
                alpha = jnp.exp2(m_prev - m_new)
                m_sub = m_new if shift_ref is None else m_new - shift_ref[x, rows, :]
                p = jnp.exp2(s - _lane_rep(m_sub, tk))
                l_ref[x, rows, :] = (alpha * l_ref[x, rows, :]
                                     + jnp.broadcast_to(jnp.sum(p, axis=-1, keepdims=True), (rb, LANES)))
                m_ref[x, rows, :] = m_new
                alpha_ref[slot, x, rows, :] = alpha
                p_ref[slot, x, rows, :] = p.astype(BF16)

    def scores(slot, t):
        for x in range(n):
            s_ref[slot, x] = qk_fns[x](t)

    def step(j, slot):
        weigh_values(1 - slot, jnp.maximum(j - 1, 0))
        softmax_tile(slot, False)
        scores(1 - slot, j + 1)

    m_ref[...] = jnp.full_like(m_ref, NEG)
    l_ref[...] = jnp.zeros_like(l_ref)
    acc_ref[...] = jnp.zeros_like(acc_ref)
    p_ref[...] = jnp.zeros_like(p_ref)
    alpha_ref[...] = jnp.ones_like(alpha_ref)

    odd = n_full % 2

    @pl.when(odd == 1)
    def _():
        scores(1, 0)
        step(0, 1)

    @pl.when(odd == 0)
    def _():
        scores(0, 0)

    def body(jj, carry):
        j = odd + 2 * jj
        step(j, 0)
        step(j + 1, 1)
        return carry

    lax.fori_loop(0, n_full // 2, body, 0)
    weigh_values(1, jnp.maximum(n_full - 1, 0))
    softmax_tile(0, True)
    weigh_values(0, n_full)


def _flash_scratch(n, tq, width):
    return [pltpu.VMEM((2, n, tq, tq), F32), pltpu.VMEM((2, n, tq, tq), BF16), pltpu.VMEM((2, n, tq, LANES), F32),
            pltpu.VMEM((n, tq, LANES), F32), pltpu.VMEM((n, tq, LANES), F32), pltpu.VMEM((n, tq, width), F32)]


def _diff_lambda(lam_ref, lam_init):
    lv = lam_ref[...]
    s1 = jnp.sum(lv[0:1, :] * lv[1:2, :], axis=-1, keepdims=True)
    s2 = jnp.sum(lv[2:3, :] * lv[3:4, :], axis=-1, keepdims=True)
    return jnp.exp(s1) - jnp.exp(s2) + lam_init


def _diff_finish(o0, o1, lam, subg, lam_init):
    o = o0 - lam * o1
    return (_rms(o, subg) * (1.0 - lam_init)).astype(BF16)


def _stick_tile(z, vc, tri, r_prev, causal):
    lsz, lk = _log_sigmoid_pair(z)
    if causal is not None:
        lk = jnp.where(causal, lk, 0.0)
    after = _dot3(lk, tri) + r_prev
    a = jnp.exp(lsz + after)
    if causal is not None:
        a = jnp.where(causal, a, 0.0)
    return _dot(a.astype(BF16), vc), r_prev + jnp.sum(lk, axis=-1, keepdims=True)


def _attn_a_prompt_kernel(lam_ref, subg_ref, q_ref, k_ref, v_ref, o_ref, s_ref, p_ref, alpha_ref, m_ref, l_ref,
                          acc_ref, *, lam_init):
    tq = q_ref.shape[0]
    i = pl.program_id(2)

    def rows_of(t):
        return pl.ds(pl.multiple_of(t * tq, tq), tq)

    def chunk_causal(row0, shape):
        ri, ci = _iota2(shape)
        return (ci // CHUNK) <= ((ri + row0) // CHUNK)

    def scores_fn(mm):
        cols = slice(mm * HEAD_DIM, (mm + 1) * HEAD_DIM)
        return lambda t: _dot_nt(q_ref[:, cols], k_ref[rows_of(t), cols])

    values_fn = lambda t: v_ref[rows_of(t), :]
    _causal_flash(i, [scores_fn(0), scores_fn(1)], [values_fn, values_fn], chunk_causal, None,
                  s_ref, p_ref, alpha_ref, m_ref, l_ref, acc_ref)
    outs = [acc_ref[mm] / _lane_rep(l_ref[mm], acc_ref.shape[2]) for mm in range(2)]
    lam = _diff_lambda(lam_ref, lam_init)
    o_ref[...] = _diff_finish(outs[0], outs[1], lam, subg_ref[...], lam_init)


def _attn_a_prompt(q, k, v, lamvec, subg, B, S, lam_init):
    tq = Q_TILE
    nq = S // tq
    w = DIFF_VDIM
    kern = functools.partial(_attn_a_prompt_kernel, lam_init=lam_init)
    return pl.pallas_call(
        kern, grid=(B, DIFF_HEADS, nq),
        in_specs=[_const_spec((4, HEAD_DIM)), _const_spec((1, w)),
                  pl.BlockSpec((tq, w), lambda b, g, i: (b * nq + i, g)),
                  pl.BlockSpec((S, w), lambda b, g, i: (b, g)),
                  pl.BlockSpec((S, w), lambda b, g, i: (b, g))],
        out_specs=pl.BlockSpec((tq, w), lambda b, g, i: (b * nq + i, g)),
        out_shape=jax.ShapeDtypeStruct((B * S, D_MODEL), BF16),
        scratch_shapes=_flash_scratch(2, tq, w),
        compiler_params=_params("arbitrary", "arbitrary", "arbitrary"), name="attn_a_prompt")(lamvec, subg, q, k, v)


def _attn_b_prompt_kernel(bias_ref, q_ref, k_ref, v_ref, o_ref):
    tq = q_ref.shape[0]
    i = pl.program_id(2)
    q0 = pl.multiple_of(i * tq, tq)
    p0 = pl.multiple_of(jnp.maximum(i - 1, 0) * tq, tq)
    q = q_ref[...]
    s_cur = _dot_nt(q, k_ref[pl.ds(q0, tq), :]) * SCALE + bias_ref[:, tq:]
    s_past = _dot_nt(q, k_ref[pl.ds(p0, tq), :]) * SCALE + bias_ref[:, :tq]
    s_past = jnp.where(i > 0, s_past, NEG)
    m = jnp.maximum(jnp.max(s_cur, axis=-1, keepdims=True), jnp.max(s_past, axis=-1, keepdims=True))
    p_cur = jnp.exp(s_cur - m)
    p_past = jnp.exp(s_past - m)
    l = jnp.sum(p_cur, axis=-1, keepdims=True) + jnp.sum(p_past, axis=-1, keepdims=True)
    o = _dot(p_cur.astype(BF16), v_ref[pl.ds(q0, tq), :]) + _dot(p_past.astype(BF16), v_ref[pl.ds(p0, tq), :])
    o_ref[...] = (o / l).astype(BF16)


def _attn_b_prompt(q, k, v, bias, B, S):
    tq = Q_TILE
    assert tq == BAND_PAST
    nq = S // tq
    d = HEAD_DIM
    return pl.pallas_call(
        _attn_b_prompt_kernel, grid=(B, N_HEADS, nq),
        in_specs=[pl.BlockSpec((None, tq, 2 * tq), lambda b, h, i: (h, 0, 0)),
                  pl.BlockSpec((tq, d), lambda b, h, i: (b * nq + i, h)),
                  pl.BlockSpec((S, d), lambda b, h, i: (b, h)),
                  pl.BlockSpec((S, d), lambda b, h, i: (b, h))],
        out_specs=pl.BlockSpec((tq, d), lambda b, h, i: (b * nq + i, h)),
        out_shape=jax.ShapeDtypeStruct((B * S, D_MODEL), BF16),
        compiler_params=_params("arbitrary", "arbitrary", "arbitrary"), name="attn_b_prompt")(bias, q, k, v)


def _attn_c_prompt_kernel(tri_ref, q_ref, k_ref, v_ref, o_ref, r_ref, acc_ref):
    tq = q_ref.shape[0]
    tk = STICK_K_TILE
    nsub = tq // tk
    i = pl.program_id(2)
    q = q_ref[...]
    tri = tri_ref[...]
    r_ref[...] = jnp.zeros_like(r_ref)
    acc_ref[...] = jnp.zeros_like(acc_ref)
    ri, ci = _iota2((tq, tk))

    def step(k0, causal):
        pv, r_new = _stick_tile(_dot_nt(q, k_ref[pl.ds(k0, tk), :]), v_ref[pl.ds(k0, tk), :], tri,
                                r_ref[...], causal)
        acc_ref[...] += pv
        r_ref[...] = r_new

    for d in reversed(range(nsub)):
        step(pl.multiple_of(i * tq + d * tk, tk), (ci + d * tk) < ri)

    def cond(c):
        kb, rmax = c
        return jnp.logical_and(kb >= 0, rmax > STICK_UNDERFLOW)

    def body(c):
        kb, _ = c
        step(pl.multiple_of(kb * tk, tk), None)
        return kb - 1, jnp.max(r_ref[...])

    lax.while_loop(cond, body, (i * nsub - 1, jnp.max(r_ref[...])))
    o_ref[...] = acc_ref[...].astype(BF16)


def _attn_c_prompt(q, k, v, tri, B, S):
    tq = Q_TILE
    nq = S // tq
    d = HEAD_DIM
    return pl.pallas_call(
        _attn_c_prompt_kernel, grid=(B, N_HEADS, nq),
        in_specs=[_const_spec((STICK_K_TILE, STICK_K_TILE)),
                  pl.BlockSpec((tq, d), lambda b, h, i: (b * nq + i, h)),
                  pl.BlockSpec((S, d), lambda b, h, i: (b, h)),
                  pl.BlockSpec((S, d), lambda b, h, i: (b, h))],
        out_specs=pl.BlockSpec((tq, d), lambda b, h, i: (b * nq + i, h)),
        out_shape=jax.ShapeDtypeStruct((B * S, D_MODEL), BF16),
        scratch_shapes=[pltpu.VMEM((tq, 1), F32), pltpu.VMEM((tq, d), F32)],
        compiler_params=_params("arbitrary", "arbitrary", "arbitrary"), name="attn_c_prompt")(tri, q, k, v)


def _attn_d_prompt_kernel(q_ref, k_ref, v_ref, cq_ref, ck_ref, o_ref, s_ref, p_ref, alpha_ref, m_ref, l_ref,
                          acc_ref, cqb_ref):
    tq = q_ref.shape[0]
    nh = cqb_ref.shape[0]
    i = pl.program_id(2)
    cqb_ref[...] = jnp.broadcast_to(cq_ref[...] * LOG2E, cqb_ref.shape)

    def rows_of(t):
        return pl.ds(pl.multiple_of(t * tq, tq), tq)

    def causal(row0, shape):
        ri, ci = _iota2(shape)
        return ci <= ri + row0

    def scores_fn(x):
        cols = slice(x * HEAD_DIM, (x + 1) * HEAD_DIM)
        return lambda t: _dot_nt(q_ref[:, cols], k_ref[rows_of(t), cols]) - ck_ref[x:x + 1, rows_of(t)] * LOG2E

    def values_fn(x):
        cols = slice(x * HEAD_DIM, (x + 1) * HEAD_DIM)
        return lambda t: v_ref[rows_of(t), cols]

    _causal_flash(i, [scores_fn(x) for x in range(nh)], [values_fn(x) for x in range(nh)], causal, cqb_ref,
                  s_ref, p_ref, alpha_ref, m_ref, l_ref, acc_ref)
    for x in range(nh):
        o_ref[:, x * HEAD_DIM:(x + 1) * HEAD_DIM] = (acc_ref[x] / l_ref[x]).astype(BF16)


def _attn_d_prompt(q, k, v, cq, ck, B, S):
    tq = Q_TILE
    nq = S // tq
    nh = FLASH_HEADS
    w = nh * HEAD_DIM
    G = N_HEADS // nh
    return pl.pallas_call(
        _attn_d_prompt_kernel, grid=(B, G, nq),
        in_specs=[pl.BlockSpec((tq, w), lambda b, g, i: (b * nq + i, g)),
                  pl.BlockSpec((S, w), lambda b, g, i: (b, g)),
                  pl.BlockSpec((S, w), lambda b, g, i: (b, g)),
                  pl.BlockSpec((None, nh, tq, 1), lambda b, g, i: (b * G + g, 0, i, 0)),
                  pl.BlockSpec((None, nh, S), lambda b, g, i: (b * G + g, 0, 0))],
        out_specs=pl.BlockSpec((tq, w), lambda b, g, i: (b * nq + i, g)),
        out_shape=jax.ShapeDtypeStruct((B * S, D_MODEL), BF16),
        scratch_shapes=_flash_scratch(nh, tq, HEAD_DIM) + [pltpu.VMEM((nh, tq, LANES), F32)],
        compiler_params=_params("arbitrary", "arbitrary", "arbitrary"), name="attn_d_prompt")(q, k, v, cq, ck)


def _softmax_two(s1, s2, v1, v2, exp=jnp.exp):
    m = jnp.maximum(jnp.max(s1, axis=-1, keepdims=True), jnp.max(s2, axis=-1, keepdims=True))
    p1 = exp(s1 - m)
    p2 = exp(s2 - m)
    l = jnp.sum(p1, axis=-1, keepdims=True) + jnp.sum(p2, axis=-1, keepdims=True)
    return (_dot(p1.astype(BF16), v1) + _dot(p2.astype(BF16), v2)) / l


def _attn_a_sample_kernel(lam_ref, subg_ref, q_ref, kn_ref, vn_ref, ck_ref, cv_ref, o_ref, *, lam_init, past_len):
    sn = q_ref.shape[0]
    P = ck_ref.shape[0]
    r1, c1 = _iota2((sn, P))
    r2, c2 = _iota2((sn, sn))
    mask1 = (c1 // CHUNK) <= ((r1 + past_len) // CHUNK)
    mask2 = ((c2 + P) // CHUNK) <= ((r2 + past_len) // CHUNK)
    v1 = cv_ref[...].astype(BF16)
    v2 = vn_ref[...]
    outs = []
    for mm in range(2):
        cols = slice(mm * HEAD_DIM, (mm + 1) * HEAD_DIM)
        q = q_ref[:, cols]
        s1 = jnp.where(mask1, _dot_nt(q, ck_ref[:, cols].astype(BF16)), NEG)
        s2 = jnp.where(mask2, _dot_nt(q, kn_ref[:, cols]), NEG)
        outs.append(_softmax_two(s1, s2, v1, v2, jnp.exp2))
    lam = _diff_lambda(lam_ref, lam_init)
    o_ref[...] = _diff_finish(outs[0], outs[1], lam, subg_ref[...], lam_init)


def _attn_b_sample_kernel(b1_ref, b2_ref, q_ref, kn_ref, vn_ref, ck_ref, cv_ref, o_ref):
    q = q_ref[...]
    s1 = _dot_nt(q, ck_ref[...].astype(BF16)) * SCALE + b1_ref[...]
    s2 = _dot_nt(q, kn_ref[...]) * SCALE + b2_ref[...]
    o_ref[...] = _softmax_two(s1, s2, cv_ref[...].astype(BF16), vn_ref[...]).astype(BF16)


def _attn_c_sample_kernel(trin_ref, tri_ref, q_ref, kn_ref, vn_ref, ck_ref, cv_ref, o_ref):
    sn = q_ref.shape[0]
    P = ck_ref.shape[0]
    tk = STICK_K_TILE
    q = q_ref[...]
    ri, ci = _iota2((sn, sn))
    acc, r = _stick_tile(_dot_nt(q, kn_ref[...]), vn_ref[...], trin_ref[...], jnp.zeros((sn, 1), F32), ci < ri)
    tri = tri_ref[...]
    for kb in reversed(range(P // tk)):
        rows = slice(kb * tk, (kb + 1) * tk)
        pv, r = _stick_tile(_dot_nt(q, ck_ref[rows, :].astype(BF16)), cv_ref[rows, :].astype(BF16), tri, r, None)
        acc = acc + pv
    o_ref[...] = acc.astype(BF16)


def _attn_d_sample_kernel(triu_ref, triun_ref, q_ref, kn_ref, vn_ref, ck_ref, cv_ref, lfp_ref, lfn_ref, o_ref):
    sn = q_ref.shape[0]
    q = q_ref[...]
    c_past = _dot3(jnp.broadcast_to(lfp_ref[...], (8, lfp_ref.shape[1])), triu_ref[...])[0:1, :]
    total = c_past[:, -1:]
    c_new = _dot3(jnp.broadcast_to(lfn_ref[...], (8, sn)), triun_ref[...])[0:1, :] + total
    ri, ci = _iota2((sn, sn))
    cq = jnp.sum(jnp.where(ri == ci, jnp.broadcast_to(c_new, (sn, sn)), 0.0), axis=-1, keepdims=True)
    s1 = _dot_nt(q, ck_ref[...].astype(BF16)) + (cq - c_past) * LOG2E
    s2 = jnp.where(ci <= ri, _dot_nt(q, kn_ref[...]) + (cq - c_new) * LOG2E, NEG)
    o_ref[...] = _softmax_two(s1, s2, cv_ref[...].astype(BF16), vn_ref[...], jnp.exp2).astype(BF16)


def _attn_sample(kern, consts, q, kn, vn, ck, cv, extra, nb, sn, P, nh, name):
    w = D_MODEL // nh
    bh = lambda b, h: (b, h)
    in_specs = [_const_spec(c.shape) for c in consts]
    in_specs += [pl.BlockSpec((sn, w), bh)] * 3 + [pl.BlockSpec((P, w), bh)] * 2
    in_specs += [pl.BlockSpec((None, 1, n), lambda b, h: (b * nh + h, 0, 0)) for _, n in extra]
    return pl.pallas_call(
        kern, grid=(nb, nh), in_specs=in_specs,
        out_specs=pl.BlockSpec((sn, w), bh),
        out_shape=jax.ShapeDtypeStruct((nb * sn, D_MODEL), BF16),
        compiler_params=_params("arbitrary", "arbitrary"), name=name)(
            *consts, q, kn, vn, ck, cv, *[a for a, _ in extra])


def _rope_tables(pos):
    inv = jnp.exp(-math.log(ROPE_THETA) * jnp.arange(ROPE_HALF, dtype=F32) * (2.0 / ROPE_DIM))
    ang = pos.astype(F32)[:, None] * inv[None, :]
    cos, sin = jnp.cos(ang), jnp.sin(ang)
    n = pos.shape[0]
    pad = jnp.zeros((n, HEAD_DIM - ROPE_DIM), F32)
    zero = jnp.zeros((n, ROPE_HALF), F32)
    c = jnp.concatenate([cos, cos, pad + 1.0], axis=1)
    sa = jnp.concatenate([-sin, zero, pad], axis=1)
    sb = jnp.concatenate([zero, sin, pad], axis=1)
    return c, sa, sb


def _band_bias(rel_bias, qpos, kpos):
    rel = jnp.clip(qpos[:, None] - kpos[None, :], -REL_CLIP, REL_CLIP) + REL_CLIP
    qc, kc = qpos[:, None] // CHUNK, kpos[None, :] // CHUNK
    mask = (kc <= qc) & (kc >= qc - BAND_CHUNKS)
    return jnp.where(mask[None], rel_bias.astype(F32)[:, rel], NEG)


def _band_bias_tile(rel_bias, tq):
    period = 3 * tq
    t = jnp.arange(period, dtype=jnp.int32)
    rel = jnp.clip(2 * tq - 1 - t, -REL_CLIP, REL_CLIP) + REL_CLIP
    u = rel_bias.astype(F32)[:, rel]
    rows = jnp.tile(u, (1, tq))[:, :tq * (period - 1)].reshape(-1, tq, period - 1)
    bias = rows[:, :, tq - 1:3 * tq - 1]
    qpos = jnp.arange(tq, dtype=jnp.int32) + tq
    kpos = jnp.arange(2 * tq, dtype=jnp.int32)
    qc, kc = qpos[:, None] // CHUNK, kpos[None, :] // CHUNK
    mask = (kc <= qc) & (kc >= qc - BAND_CHUNKS)
    return jnp.where(mask[None], bias, NEG)


def _tri(n, fn):
    i = jnp.arange(n)
    return fn(i[:, None], i[None, :]).astype(BF16)


def _rows_per_head(a, nb, n):
    return jnp.transpose(a[:, :N_HEADS].reshape(nb, n, N_HEADS), (0, 2, 1)).reshape(nb * N_HEADS, 1, n)


def kernel(x_prompt, x_sample, cache_a_k, cache_a_v, cache_b_k, cache_b_v, cache_c_k, cache_c_v, cache_d_k, cache_d_v, cache_d_logf, norm_g, w_up, w_down, a_w_in, a_lam_q1, a_lam_k1, a_lam_q2, a_lam_k2, a_sub_g, a_w_out, b_w_in, b_rel_bias, b_w_out, c_w_in, c_w_out, d_w_in, d_b_f, d_w_out):
    B, S, _ = x_prompt.shape
    NB, SN, _ = x_sample.shape
    P = cache_a_k.shape[2]
    PB = cache_b_k.shape[2]
    depth = norm_g.shape[0]
    H = N_HEADS
    tm = TOKEN_TILE
    nblk = S // tm
    assert NB * SN == tm and S % Q_TILE == 0 and PB == BAND_PAST

    xp = x_prompt.reshape(B * S, D_MODEL)
    xs = x_sample.reshape(NB * SN, D_MODEL)
    pos_p = jnp.arange(S, dtype=jnp.int32)
    pos_s = P + jnp.arange(SN, dtype=jnp.int32)

    flat = lambda c: c.reshape(c.shape[0] * c.shape[1], D_MODEL)
    tri_strict = _tri(STICK_K_TILE, lambda r, c: r > c)
    st = [([], []) for _ in range(N_MIXERS)]

    for i in range(depth):
        m, r = i % N_MIXERS, i // N_MIXERS
        g = norm_g[i].astype(F32)
        if m == 0:
            lam_init = 0.8 - 0.6 * math.exp(-0.3 * i)
            w = a_w_in[r].astype(BF16)
            lamvec = jnp.stack([a_lam_q1[r], a_lam_k1[r], a_lam_q2[r], a_lam_k2[r]]).astype(F32)
            subg = a_sub_g[r].astype(F32).reshape(1, DIFF_VDIM)
            qb, kf, kb, vf, vb = _proj_in(xp, g[0], w, nblk=nblk, rope=_rope_tables(pos_p), scale_q=SCALE_LOG2E)
            op = _attn_a_prompt(qb, kb, vb, lamvec, subg, B, S, lam_init)
            sp = (kf.reshape(B, S, H, HEAD_DIM), vf.reshape(B, S, DIFF_HEADS, DIFF_VDIM))
            rope_s = tuple(jnp.tile(t, (NB, 1)) for t in _rope_tables(pos_s))
            qb, kf, kb, vf, vb = _proj_in(xs, g[0], w, nblk=1, rope=rope_s, scale_q=SCALE_LOG2E)
            kern = functools.partial(_attn_a_sample_kernel, lam_init=lam_init, past_len=P)
            os_ = _attn_sample(kern, [lamvec, subg], qb, kb, vb, flat(cache_a_k[r]), flat(cache_a_v[r]), [],
                               NB, SN, P, DIFF_HEADS, "attn_a_sample")
            ss = (kf.reshape(NB, SN, H, HEAD_DIM), vf.reshape(NB, SN, DIFF_HEADS, DIFF_VDIM))
            w_out = a_w_out[r]
        elif m == 1:
            w = b_w_in[r].astype(BF16)
            qb, kb, vb, kl, vl = _proj_in(xp, g[0], w, nblk=nblk, kv_last=True)
            op = _attn_b_prompt(qb, kb, vb, _band_bias_tile(b_rel_bias[r], Q_TILE), B, S)
            sp = (kl.reshape(B, PB, H, HEAD_DIM), vl.reshape(B, PB, H, HEAD_DIM))
            qb, kb, vb, kl, vl = _proj_in(xs, g[0], w, nblk=1, kv_last=True)
            kpos = P - PB + jnp.arange(PB + SN, dtype=jnp.int32)
            kvis = _band_bias(b_rel_bias[r], pos_s, kpos)
            kvis = jnp.where((kpos >= 0)[None, None, :], kvis, NEG)
            b1 = kvis[:, :, :PB].reshape(H * SN, PB)
            b2 = kvis[:, :, PB:].reshape(H * SN, SN)
            os_ = _attn_b_sample(qb, kb, vb, flat(cache_b_k[r]), flat(cache_b_v[r]), b1, b2, NB, SN, PB)
            kn = kl.reshape(NB, SN, H, HEAD_DIM)
            vn = vl.reshape(NB, SN, H, HEAD_DIM)
            ss = (jnp.concatenate([cache_b_k[r], kn], axis=1)[:, SN:], jnp.concatenate([cache_b_v[r], vn], axis=1)[:, SN:])
            w_out = b_w_out[r]
        elif m == 2:
            w = c_w_in[r].astype(BF16)
            qb, kf, kb, vf, vb = _proj_in(xp, g[0], w, nblk=nblk, scale_q=SCALE)
            op = _attn_c_prompt(qb, kb, vb, tri_strict, B, S)
            sp = (kf.reshape(B, S, H, HEAD_DIM), vf.reshape(B, S, H, HEAD_DIM))
            qb, kf, kb, vf, vb = _proj_in(xs, g[0], w, nblk=1, scale_q=SCALE)
            os_ = _attn_sample(_attn_c_sample_kernel, [_tri(SN, lambda r_, c_: r_ > c_), tri_strict], qb, kb, vb,
                               flat(cache_c_k[r]), flat(cache_c_v[r]), [], NB, SN, P, H, "attn_c_sample")
            ss = (kf.reshape(NB, SN, H, HEAD_DIM), vf.reshape(NB, SN, H, HEAD_DIM))
            w_out = c_w_out[r]
        else:
            w_all = d_w_in[r]
            w = w_all[:, :3 * D_MODEL].astype(BF16)
            wf = jnp.pad(w_all[:, 3 * D_MODEL:], ((0, 0), (0, LANES - H))).astype(BF16)
            bf = jnp.pad(d_b_f[r].astype(F32), (0, LANES - H)).reshape(1, LANES)
            fox = (wf, bf, _tri(tm, lambda r_, c_: r_ >= c_))
            qb, kf, kb, vf, vb, lf, c = _proj_in(xp, g[0], w, nblk=nblk, scale_q=SCALE_LOG2E, fox=fox)
            ck = _rows_per_head(c, B, S).reshape(B * H // FLASH_HEADS, FLASH_HEADS, S)
            op = _attn_d_prompt(qb, kb, vb, ck[..., None], ck, B, S)
            sp = (kf.reshape(B, S, H, HEAD_DIM), vf.reshape(B, S, H, HEAD_DIM), lf[:, :H].reshape(B, S, H))
            qb, kf, kb, vf, vb, lf, c = _proj_in(xs, g[0], w, nblk=1, scale_q=SCALE_LOG2E, fox=fox)
            lfp = jnp.transpose(cache_d_logf[r].astype(F32), (0, 2, 1)).reshape(NB * H, 1, P)
            lfn = _rows_per_head(lf, NB, SN)
            triu = [_tri(P, lambda r_, c_: r_ <= c_), _tri(SN, lambda r_, c_: r_ <= c_)]
            os_ = _attn_sample(_attn_d_sample_kernel, triu, qb, kb, vb, flat(cache_d_k[r]), flat(cache_d_v[r]),
                               [(lfp, P), (lfn, SN)], NB, SN, P, H, "attn_d_sample")
            ss = (kf.reshape(NB, SN, H, HEAD_DIM), vf.reshape(NB, SN, H, HEAD_DIM), lf[:, :H].reshape(NB, SN, H))
            w_out = d_w_out[r]
        st[m][0].append(sp)
        st[m][1].append(ss)
        wo, wu, wd = w_out.astype(BF16), w_up[i].astype(BF16), w_down[i].astype(BF16)
        xp = _out_mlp(op, xp, g[1:4], wo, wu, wd)
        xs = _out_mlp(os_, xs, g[1:4], wo, wu, wd)

    stack = lambda states, j: jnp.stack([s[j] for s in states], axis=0)
    pa, sa = st[0]
    pb, sb = st[1]
    pc, sc = st[2]
    pd, sd = st[3]
    return (xp.reshape(B, S, D_MODEL), xs.reshape(NB, SN, D_MODEL),
            stack(pa, 0), stack(pa, 1), stack(sa, 0), stack(sa, 1),
            stack(pb, 0), stack(pb, 1), stack(sb, 0), stack(sb, 1),
            stack(pc, 0), stack(pc, 1), stack(sc, 0), stack(sc, 1),
            stack(pd, 0), stack(pd, 1), stack(pd, 2), stack(sd, 0), stack(sd, 1), stack(sd, 2))


def _attn_b_sample(q, kn, vn, ck, cv, b1, b2, nb, sn, pb):
    d = HEAD_DIM
    bh = lambda b, h: (b, h)
    return pl.pallas_call(
        _attn_b_sample_kernel, grid=(nb, N_HEADS),
        in_specs=[pl.BlockSpec((sn, pb), lambda b, h: (h, 0)), pl.BlockSpec((sn, sn), lambda b, h: (h, 0)),
                  pl.BlockSpec((sn, d), bh), pl.BlockSpec((sn, d), bh), pl.BlockSpec((sn, d), bh),
                  pl.BlockSpec((pb, d), bh), pl.BlockSpec((pb, d), bh)],
        out_specs=pl.BlockSpec((sn, d), bh),
        out_shape=jax.ShapeDtypeStruct((nb * sn, D_MODEL), BF16),
        compiler_params=_params("arbitrary", "arbitrary"), name="attn_b_sample")(b1, b2, q, kn, vn, ck, cv)
```

```python
import functools
import math

import jax
import jax.numpy as jnp
from jax import lax
from jax.experimental import pallas as pl
from jax.experimental.pallas import tpu as pltpu

D_MODEL = 1024
HEAD_DIM = 128
N_HEADS = D_MODEL // HEAD_DIM
DIFF_HEADS = N_HEADS // 2
DIFF_VDIM = 2 * HEAD_DIM
N_MIXERS = 4
CHUNK = 64
ROPE_THETA = 500000.0
ROPE_DIM = HEAD_DIM // 4
ROPE_HALF = ROPE_DIM // 2
BAND_CHUNKS = 8
BAND_PAST = BAND_CHUNKS * CHUNK
REL_CLIP = 128
D_FF = 4 * D_MODEL
NORM_EPS = 1e-6
SCALE = HEAD_DIM ** -0.5
LOG2E = math.log2(math.e)
SCALE_LOG2E = SCALE * LOG2E
NEG = -1e30

LANES = 128
TOKEN_TILE = 512
Q_TILE = 512
FLASH_ROWS = 128
FLASH_HEADS = 2
BAND_ROWS = 128
BAND_WINDOW = BAND_PAST + BAND_ROWS
STICK_K_TILE = 256
STICK_UNDERFLOW = -104.0
FLASH_UNDERFLOW = -152.0
NORM_MARGIN = 1.02
VMEM_LIMIT_BYTES = 56 * 1024 * 1024

BF16 = jnp.bfloat16
F32 = jnp.float32


def _dot(a, b):
    return jnp.dot(a, b, preferred_element_type=F32)


def _dot_nt(a, b):
    return lax.dot_general(a, b, (((1,), (1,)), ((), ())), preferred_element_type=F32)


def _rms(xf, g):
    ms = jnp.mean(xf * xf, axis=-1, keepdims=True)
    return xf * lax.rsqrt(ms + NORM_EPS) * g


def _split3(x):
    hi = x.astype(BF16)
    r = x - hi.astype(F32)
    mid = r.astype(BF16)
    lo = (r - mid.astype(F32)).astype(BF16)
    return hi, mid, lo


def _dot3(x, t):
    hi, mid, lo = _split3(x)
    return _dot(hi, t) + _dot(mid, t) + _dot(lo, t)


def _dot3_t(t, x):
    hi, mid, lo = _split3(x)
    return _dot(t, hi) + _dot(t, mid) + _dot(t, lo)


def _log_sigmoid_pair(z):
    t = jnp.log1p(jnp.exp(-jnp.abs(z)))
    return jnp.minimum(z, 0.0) - t, -jnp.maximum(z, 0.0) - t


def _params(*sem):
    return pltpu.CompilerParams(dimension_semantics=sem, vmem_limit_bytes=VMEM_LIMIT_BYTES)


def _const_spec(shape):
    nd = len(shape)
    return pl.BlockSpec(shape, lambda *_: (0,) * nd, pipeline_mode=pl.Buffered(1))


def _proj_in_kernel(*refs, rope, scale_q, kv_last, fox, nblk):
    it = iter(refs)
    x_ref, g_ref, w_ref = next(it), next(it), next(it)
    if rope:
        cos_ref, sa_ref, sb_ref = next(it), next(it), next(it)
    if fox:
        wf_ref, bf_ref, tril_ref, heads_ref = next(it), next(it), next(it), next(it)
    qb_ref = next(it)
    if kv_last:
        kb_ref, vb_ref, kf_ref, vf_ref = next(it), next(it), next(it), next(it)
    else:
        kf_ref, kb_ref, vf_ref, vb_ref = next(it), next(it), next(it), next(it)
    if fox:
        lf_ref, c_ref, norm_ref, carry_ref = next(it), next(it), next(it), next(it)

    i = pl.program_id(0)
    h = _rms(x_ref[...], g_ref[...]).astype(BF16)
    cw = 4 * HEAD_DIM
    is_last = (i % nblk) == (nblk - 1)

    def rotate(xh):
        return (xh * cos_ref[...] + pltpu.roll(xh, HEAD_DIM - ROPE_HALF, 1) * sa_ref[...]
                + pltpu.roll(xh, ROPE_HALF, 1) * sb_ref[...])

    for part in range(3):
        for c in range(D_MODEL // cw):
            col = c * cw
            acc = _dot(h, w_ref[:, part * D_MODEL + col:part * D_MODEL + col + cw])
            if rope and part < 2:
                acc = jnp.concatenate(
                    [rotate(acc[:, hh * HEAD_DIM:(hh + 1) * HEAD_DIM]) for hh in range(cw // HEAD_DIM)], axis=1)
            if part == 0:
                if scale_q is not None:
                    acc = acc * scale_q
                qb_ref[:, col:col + cw] = acc.astype(BF16)
            else:
                fref, bref = (kf_ref, kb_ref) if part == 1 else (vf_ref, vb_ref)
                bref[:, col:col + cw] = acc.astype(BF16)
                if kv_last:
                    @pl.when(is_last)
                    def _():
                        fref[:, col:col + cw] = acc
                else:
                    fref[:, col:col + cw] = acc

    if fox:
        lf, _ = _log_sigmoid_pair(_dot(h, wf_ref[...]) + bf_ref[...])

        @pl.when((i % nblk) == 0)
        def _():
            carry_ref[...] = jnp.zeros_like(carry_ref)

        c_tile = _dot3_t(tril_ref[...], lf) + carry_ref[0:1, :]
        lf_ref[...] = lf
        c_ref[...] = c_tile
        carry_ref[...] = jnp.broadcast_to(c_tile[-1:, :], carry_ref.shape)

        def max_sq_norm(ref):
            x = ref[...].astype(F32)
            sq = _dot((x * x).astype(BF16), heads_ref[...])
            return jnp.broadcast_to(jnp.max(sq, axis=0, keepdims=True), (8, LANES))

        norm_ref[0:8, :] = max_sq_norm(qb_ref)
        norm_ref[8:16, :] = max_sq_norm(kb_ref)


def _proj_in(x, g, w, *, nblk, rope=None, scale_q=None, kv_last=False, fox=None):
    T = x.shape[0]
    tm = TOKEN_TILE
    assert T % tm == 0 and (T // tm) % nblk == 0
    n = T // tm
    row = lambda i: (i, 0)
    in_specs = [pl.BlockSpec((tm, D_MODEL), row), _const_spec((1, D_MODEL)), _const_spec((D_MODEL, 3 * D_MODEL))]
    args = [x, g.reshape(1, D_MODEL), w]
    if rope is not None:
        tab = lambda i: (i % nblk, 0)
        in_specs += [pl.BlockSpec((tm, HEAD_DIM), tab)] * 3
        args += list(rope)
    if fox is not None:
        in_specs += [_const_spec((D_MODEL, LANES)), _const_spec((1, LANES)), _const_spec((tm, tm)),
                     _const_spec((D_MODEL, LANES))]
        args += list(fox)
    full_b = jax.ShapeDtypeStruct((T, D_MODEL), BF16)
    full_f = jax.ShapeDtypeStruct((T, D_MODEL), F32)
    tile = pl.BlockSpec((tm, D_MODEL), row)
    if kv_last:
        last_f = jax.ShapeDtypeStruct((T // nblk, D_MODEL), F32)
        last = pl.BlockSpec((tm, D_MODEL), lambda i: (i // nblk, 0))
        out_shape = [full_b, full_b, full_b, last_f, last_f]
        out_specs = [tile, tile, tile, last, last]
    else:
        out_shape = [full_b, full_f, full_b, full_f, full_b]
        out_specs = [tile] * 5
    scratch = []
    if fox is not None:
        out_shape += [jax.ShapeDtypeStruct((T, LANES), F32)] * 2 + [jax.ShapeDtypeStruct((n * 16, LANES), F32)]
        out_specs += [pl.BlockSpec((tm, LANES), row)] * 2 + [pl.BlockSpec((16, LANES), row)]
        scratch = [pltpu.VMEM((8, LANES), F32)]
    kern = functools.partial(_proj_in_kernel, rope=rope is not None, scale_q=scale_q, kv_last=kv_last,
                             fox=fox is not None, nblk=nblk)
    return pl.pallas_call(
        kern, grid=(n,), in_specs=in_specs, out_specs=out_specs, out_shape=out_shape,
        scratch_shapes=scratch, compiler_params=_params("arbitrary"), name="proj_in")(*args)


def _out_mlp_kernel(o_ref, x_ref, g_ref, wo_ref, wu_ref, wd_ref, y_ref):
    a = _dot(o_ref[...], wo_ref[...])
    x1 = x_ref[...] + _rms(a, g_ref[0:1, :])
    h2 = _rms(x1, g_ref[1:2, :]).astype(BF16)
    acc = jnp.zeros_like(x1)
    for c in range(D_FF // D_MODEL):
        u = jnp.maximum(_dot(h2, wu_ref[:, c * D_MODEL:(c + 1) * D_MODEL]), 0.0)
        acc = acc + _dot((u * u).astype(BF16), wd_ref[c * D_MODEL:(c + 1) * D_MODEL, :])
    y_ref[...] = x1 + _rms(acc, g_ref[2:3, :])


def _out_mlp(o, x, g3, wo, wu, wd):
    T = x.shape[0]
    tm = TOKEN_TILE
    row = lambda i: (i, 0)
    return pl.pallas_call(
        _out_mlp_kernel, grid=(T // tm,),
        in_specs=[pl.BlockSpec((tm, D_MODEL), row), pl.BlockSpec((tm, D_MODEL), row), _const_spec((3, D_MODEL)),
                  _const_spec((D_MODEL, D_MODEL)), _const_spec((D_MODEL, D_FF)), _const_spec((D_FF, D_MODEL))],
        out_specs=pl.BlockSpec((tm, D_MODEL), row),
        out_shape=jax.ShapeDtypeStruct((T, D_MODEL), F32),
        compiler_params=_params("arbitrary"), name="out_mlp")(o, x, g3, wo, wu, wd)


def _lane_rep(x, width):
    return x if width == LANES else jnp.concatenate([x] * (width // LANES), axis=1)


def _iota2(shape):
    return lax.broadcasted_iota(jnp.int32, shape, 0), lax.broadcasted_iota(jnp.int32, shape, 1)


def _causal_flash(first, last, qk_fns, v_fns, diag_mask_fn, shift_ref, s_ref, p_ref, alpha_ref, m_ref, l_ref,
                  acc_ref):
    _, n, tq, tk = s_ref.shape
    rb = FLASH_ROWS
    width = acc_ref.shape[2]

    def weigh_values(slot, t):
        for x in range(n):
            acc_ref[x] = _lane_rep(alpha_ref[slot, x], width) * acc_ref[x] + _dot(p_ref[slot, x], v_fns[x](t))

    def softmax_tile(slot, diagonal):
        for x in range(n):
            for r in range(tq // rb):
                rows = slice(r * rb, (r + 1) * rb)
                s = s_ref[slot, x, rows, :]
                if diagonal:
                    s = jnp.where(diag_mask_fn(r * rb, (rb, tk)), s, NEG)
                m_prev = m_ref[x, rows, :]
                m_cur = jnp.broadcast_to(jnp.max(s, axis=-1, keepdims=True), (rb, LANES))
                if shift_ref is not None:
                    m_cur = m_cur + shift_ref[x, rows, :]
                m_new = jnp.maximum(m_prev, m_cur)
                alpha = jnp.exp2(m_prev - m_new)
                m_sub = m_new if shift_ref is None else m_new - shift_ref[x, rows, :]
                p = jnp.exp2(s - _lane_rep(m_sub, tk))
                l_ref[x, rows, :] = (alpha * l_ref[x, rows, :]
                                     + jnp.broadcast_to(jnp.sum(p, axis=-1, keepdims=True), (rb, LANES)))
                m_ref[x, rows, :] = m_new
                alpha_ref[slot, x, rows, :] = alpha
                p_ref[slot, x, rows, :] = p.astype(BF16)

    def scores(slot, t):
        for x in range(n):
            s_ref[slot, x] = qk_fns[x](t)

    def step(j, slot):
        weigh_values(1 - slot, jnp.maximum(j - 1, first))
        softmax_tile(slot, False)
        scores(1 - slot, j + 1)

    m_ref[...] = jnp.full_like(m_ref, NEG)
    l_ref[...] = jnp.zeros_like(l_ref)
    acc_ref[...] = jnp.zeros_like(acc_ref)
    p_ref[...] = jnp.zeros_like(p_ref)
    alpha_ref[...] = jnp.ones_like(alpha_ref)

    n_full = last - first
    odd = n_full % 2

    @pl.when(odd == 1)
    def _():
        scores(1, first)
        step(first, 1)

    @pl.when(odd == 0)
    def _():
        scores(0, first)

    def body(jj, carry):
        j = first + odd + 2 * jj
        step(j, 0)
        step(j + 1, 1)
        return carry

    lax.fori_loop(0, n_full // 2, body, 0)
    weigh_values(1, jnp.maximum(last - 1, first))
    softmax_tile(0, True)
    weigh_values(0, last)


def _flash_scratch(n, tq, width):
    return [pltpu.VMEM((2, n, tq, tq), F32), pltpu.VMEM((2, n, tq, tq), BF16), pltpu.VMEM((2, n, tq, LANES), F32),
            pltpu.VMEM((n, tq, LANES), F32), pltpu.VMEM((n, tq, LANES), F32), pltpu.VMEM((n, tq, width), F32)]


def _diff_lambda(lam_ref, lam_init):
    lv = lam_ref[...]
    s1 = jnp.sum(lv[0:1, :] * lv[1:2, :], axis=-1, keepdims=True)
    s2 = jnp.sum(lv[2:3, :] * lv[3:4, :], axis=-1, keepdims=True)
    return jnp.exp(s1) - jnp.exp(s2) + lam_init


def _diff_finish(o0, o1, lam, subg, lam_init):
    o = o0 - lam * o1
    return (_rms(o, subg) * (1.0 - lam_init)).astype(BF16)


def _stick_tile(z, vc, tri, r_prev, causal):
    lsz, lk = _log_sigmoid_pair(z)
    if causal is not None:
        lk = jnp.where(causal, lk, 0.0)
    after = _dot3(lk, tri) + r_prev
    a = jnp.exp(lsz + after)
    if causal is not None:
        a = jnp.where(causal, a, 0.0)
    return _dot(a.astype(BF16), vc), r_prev + jnp.sum(lk, axis=-1, keepdims=True)


def _attn_a_prompt_kernel(lam_ref, subg_ref, q_ref, k_ref, v_ref, o_ref, s_ref, p_ref, alpha_ref, m_ref, l_ref,
                          acc_ref, *, lam_init):
    tq = q_ref.shape[0]
    i = pl.program_id(2)

    def rows_of(t):
        return pl.ds(pl.multiple_of(t * tq, tq), tq)

    def chunk_causal(row0, shape):
        ri, ci = _iota2(shape)
        return (ci // CHUNK) <= ((ri + row0) // CHUNK)

    def scores_fn(mm):
        cols = slice(mm * HEAD_DIM, (mm + 1) * HEAD_DIM)
        return lambda t: _dot_nt(q_ref[:, cols], k_ref[rows_of(t), cols])

    values_fn = lambda t: v_ref[rows_of(t), :]
    _causal_flash(0, i, [scores_fn(0), scores_fn(1)], [values_fn, values_fn], chunk_causal, None,
                  s_ref, p_ref, alpha_ref, m_ref, l_ref, acc_ref)
    outs = [acc_ref[mm] / _lane_rep(l_ref[mm], acc_ref.shape[2]) for mm in range(2)]
    lam = _diff_lambda(lam_ref, lam_init)
    o_ref[...] = _diff_finish(outs[0], outs[1], lam, subg_ref[...], lam_init)


def _attn_a_prompt(q, k, v, lamvec, subg, B, S, lam_init):
    tq = Q_TILE
    nq = S // tq
    w = DIFF_VDIM
    kern = functools.partial(_attn_a_prompt_kernel, lam_init=lam_init)
    return pl.pallas_call(
        kern, grid=(B, DIFF_HEADS, nq),
        in_specs=[_const_spec((4, HEAD_DIM)), _const_spec((1, w)),
                  pl.BlockSpec((tq, w), lambda b, g, i: (b * nq + i, g)),
                  pl.BlockSpec((S, w), lambda b, g, i: (b, g)),
                  pl.BlockSpec((S, w), lambda b, g, i: (b, g))],
        out_specs=pl.BlockSpec((tq, w), lambda b, g, i: (b * nq + i, g)),
        out_shape=jax.ShapeDtypeStruct((B * S, D_MODEL), BF16),
        scratch_shapes=_flash_scratch(2, tq, w),
        compiler_params=_params("arbitrary", "arbitrary", "arbitrary"), name="attn_a_prompt")(lamvec, subg, q, k, v)


def _attn_b_prompt_kernel(bias_ref, q_ref, k_ref, v_ref, o_ref):
    tq = q_ref.shape[0]
    rb, nwin = bias_ref.shape
    i = pl.program_id(2)

    def row_block(r, start, nk):
        rows = slice(r * rb, (r + 1) * rb)
        s = _dot_nt(q_ref[rows, :], k_ref[pl.ds(start, nk), :]) * SCALE + bias_ref[:, nwin - nk:]
        p = jnp.exp(s - jnp.max(s, axis=-1, keepdims=True))
        l = jnp.sum(p, axis=-1, keepdims=True)
        o_ref[rows, :] = (_dot(p.astype(BF16), v_ref[pl.ds(start, nk), :]) / l).astype(BF16)

    @pl.when(i > 0)
    def _():
        for r in range(tq // rb):
            row_block(r, pl.multiple_of(i * tq + (r + 1) * rb - nwin, rb), nwin)

    @pl.when(i == 0)
    def _():
        for r in range(tq // rb):
            row_block(r, 0, (r + 1) * rb)


def _attn_b_prompt(q, k, v, bias, B, S):
    tq = Q_TILE
    assert tq + BAND_ROWS >= BAND_WINDOW
    nq = S // tq
    d = HEAD_DIM
    return pl.pallas_call(
        _attn_b_prompt_kernel, grid=(B, N_HEADS, nq),
        in_specs=[pl.BlockSpec((None, BAND_ROWS, BAND_WINDOW), lambda b, h, i: (h, 0, 0)),
                  pl.BlockSpec((tq, d), lambda b, h, i: (b * nq + i, h)),
                  pl.BlockSpec((S, d), lambda b, h, i: (b, h)),
                  pl.BlockSpec((S, d), lambda b, h, i: (b, h))],
        out_specs=pl.BlockSpec((tq, d), lambda b, h, i: (b * nq + i, h)),
        out_shape=jax.ShapeDtypeStruct((B * S, D_MODEL), BF16),
        compiler_params=_params("arbitrary", "arbitrary", "arbitrary"), name="attn_b_prompt")(bias, q, k, v)


def _attn_c_prompt_kernel(tri_ref, q_ref, k_ref, v_ref, o_ref, r_ref, acc_ref):
    tq = q_ref.shape[0]
    tk = STICK_K_TILE
    assert tq == 2 * tk
    i = pl.program_id(2)
    q = q_ref[...]
    tri = tri_ref[...]
    lo, hi = slice(0, tk), slice(tk, tq)
    ri, ci = _iota2((tk, tk))
    below_diag = ci < ri

    def keys(kb):
        return pl.ds(pl.multiple_of(kb * tk, tk), tk)

    def parts(qrows, krows, causal, valid=None):
        lsz, lk = _log_sigmoid_pair(_dot_nt(qrows, k_ref[krows, :]))
        if causal is not None:
            lk = jnp.where(causal, lk, 0.0)
            lsz = jnp.where(causal, lsz, NEG)
        if valid is not None:
            lk = jnp.where(valid, lk, 0.0)
            lsz = jnp.where(valid, lsz, NEG)
        return lsz, _dot3(lk, tri), jnp.sum(lk, axis=-1, keepdims=True)

    def weigh(lsz, within, r_right, krows):
        return _dot(jnp.exp(lsz + within + r_right).astype(BF16), v_ref[krows, :])

    k_new, k_mid, k_old = keys(2 * i + 1), keys(2 * i), keys(jnp.maximum(2 * i - 1, 0))
    lsz_a, within_a, tot_a = parts(q[hi], k_new, below_diag)
    lsz_b0, within_b0, tot_b0 = parts(q[lo], k_mid, below_diag)
    lsz_b1, within_b1, tot_b1 = parts(q[hi], k_mid, None)
    lsz_c, within_c, tot_c = parts(q, k_old, None, valid=i > 0)
    zero = jnp.zeros((tk, 1), F32)
    r_b = jnp.concatenate([zero, tot_a], axis=0)
    r_c = r_b + jnp.concatenate([tot_b0, tot_b1], axis=0)
    acc_ref[lo, :] = weigh(lsz_b0, within_b0, zero, k_mid) + weigh(lsz_c[lo], within_c[lo], r_c[lo], k_old)
    acc_ref[hi, :] = (weigh(lsz_a, within_a, zero, k_new) + weigh(lsz_b1, within_b1, tot_a, k_mid)
                      + weigh(lsz_c[hi], within_c[hi], r_c[hi], k_old))
    r_ref[...] = r_c + tot_c

    def cond(c):
        kb, rmax = c
        return jnp.logical_and(kb >= 0, rmax > STICK_UNDERFLOW)

    def body(c):
        kb, _ = c
        lsz, within, tot = parts(q, keys(kb), None)
        acc_ref[...] += weigh(lsz, within, r_ref[...], keys(kb))
        r_ref[...] += tot
        return kb - 1, jnp.max(r_ref[...])

    lax.while_loop(cond, body, (2 * i - 2, jnp.max(r_ref[...])))
    o_ref[...] = acc_ref[...].astype(BF16)


def _attn_c_prompt(q, k, v, tri, B, S):
    tq = Q_TILE
    nq = S // tq
    d = HEAD_DIM
    return pl.pallas_call(
        _attn_c_prompt_kernel, grid=(B, N_HEADS, nq),
        in_specs=[_const_spec((STICK_K_TILE, STICK_K_TILE)),
                  pl.BlockSpec((tq, d), lambda b, h, i: (b * nq + i, h)),
                  pl.BlockSpec((S, d), lambda b, h, i: (b, h)),
                  pl.BlockSpec((S, d), lambda b, h, i: (b, h))],
        out_specs=pl.BlockSpec((tq, d), lambda b, h, i: (b * nq + i, h)),
        out_shape=jax.ShapeDtypeStruct((B * S, D_MODEL), BF16),
        scratch_shapes=[pltpu.VMEM((tq, 1), F32), pltpu.VMEM((tq, d), F32)],
        compiler_params=_params("arbitrary", "arbitrary", "arbitrary"), name="attn_c_prompt")(tri, q, k, v)


def _attn_d_prompt_kernel(first_ref, q_ref, k_ref, v_ref, cq_ref, ck_ref, o_ref, s_ref, p_ref, alpha_ref, m_ref,
                          l_ref, acc_ref, cqb_ref):
    tq = q_ref.shape[0]
    nh = cqb_ref.shape[0]
    i = pl.program_id(2)
    first = first_ref[(pl.program_id(0) * pl.num_programs(1) + pl.program_id(1)) * pl.num_programs(2) + i]
    cqb_ref[...] = jnp.broadcast_to(cq_ref[...] * LOG2E, cqb_ref.shape)

    def rows_of(t):
        return pl.ds(pl.multiple_of(t * tq, tq), tq)

    def causal(row0, shape):
        ri, ci = _iota2(shape)
        return ci <= ri + row0

    def scores_fn(x):
        cols = slice(x * HEAD_DIM, (x + 1) * HEAD_DIM)
        return lambda t: _dot_nt(q_ref[:, cols], k_ref[rows_of(t), cols]) - ck_ref[x:x + 1, rows_of(t)] * LOG2E

    def values_fn(x):
        cols = slice(x * HEAD_DIM, (x + 1) * HEAD_DIM)
        return lambda t: v_ref[rows_of(t), cols]

    _causal_flash(first, i, [scores_fn(x) for x in range(nh)], [values_fn(x) for x in range(nh)], causal, cqb_ref,
                  s_ref, p_ref, alpha_ref, m_ref, l_ref, acc_ref)
    for x in range(nh):
        o_ref[:, x * HEAD_DIM:(x + 1) * HEAD_DIM] = (acc_ref[x] / l_ref[x]).astype(BF16)


def _first_live_tile(norms, c, B, S):
    tq = Q_TILE
    nq = S // tq
    H = N_HEADS
    nr = jnp.sqrt(norms.reshape(B, nq, 2, 8, LANES)[:, :, :, 0, :H]) * NORM_MARGIN
    qn, kn = nr[:, :, 0], nr[:, :, 1]
    ct = c[:, :H].reshape(B, nq, tq, H) * LOG2E
    cmax, cmin = jnp.max(ct, axis=2), jnp.min(ct, axis=2)
    bound = (qn[:, :, None] * (kn[:, None, :] + kn[:, :, None])
             + cmax[:, :, None] - cmin[:, None, :])
    t = jnp.arange(nq)
    live = (bound >= FLASH_UNDERFLOW) & (t[None, None, :, None] <= t[None, :, None, None])
    live = live.reshape(B, nq, nq, H // FLASH_HEADS, FLASH_HEADS).any(axis=-1)
    first = jnp.argmax(live, axis=2).astype(jnp.int32)
    return jnp.transpose(first, (0, 2, 1)).reshape(-1)


def _attn_d_prompt(q, k, v, cq, ck, first, B, S):
    tq = Q_TILE
    nq = S // tq
    nh = FLASH_HEADS
    w = nh * HEAD_DIM
    G = N_HEADS // nh
    return pl.pallas_call(
        _attn_d_prompt_kernel, grid=(B, G, nq),
        in_specs=[pl.BlockSpec(memory_space=pltpu.SMEM),
                  pl.BlockSpec((tq, w), lambda b, g, i: (b * nq + i, g)),
                  pl.BlockSpec((S, w), lambda b, g, i: (b, g)),
                  pl.BlockSpec((S, w), lambda b, g, i: (b, g)),
                  pl.BlockSpec((None, nh, tq, 1), lambda b, g, i: (b * G + g, 0, i, 0)),
                  pl.BlockSpec((None, nh, S), lambda b, g, i: (b * G + g, 0, 0))],
        out_specs=pl.BlockSpec((tq, w), lambda b, g, i: (b * nq + i, g)),
        out_shape=jax.ShapeDtypeStruct((B * S, D_MODEL), BF16),
        scratch_shapes=_flash_scratch(nh, tq, HEAD_DIM) + [pltpu.VMEM((nh, tq, LANES), F32)],
        compiler_params=_params("arbitrary", "arbitrary", "arbitrary"), name="attn_d_prompt")(
            first, q, k, v, cq, ck)


def _softmax_two(s1, s2, v1, v2, exp=jnp.exp):
    m = jnp.maximum(jnp.max(s1, axis=-1, keepdims=True), jnp.max(s2, axis=-1, keepdims=True))
    p1 = exp(s1 - m)
    p2 = exp(s2 - m)
    l = jnp.sum(p1, axis=-1, keepdims=True) + jnp.sum(p2, axis=-1, keepdims=True)
    return (_dot(p1.astype(BF16), v1) + _dot(p2.astype(BF16), v2)) / l


def _attn_a_sample_kernel(lam_ref, subg_ref, q_ref, kn_ref, vn_ref, ck_ref, cv_ref, o_ref, *, lam_init, past_len):
    sn = q_ref.shape[0]
    P = ck_ref.shape[0]
    r1, c1 = _iota2((sn, P))
    r2, c2 = _iota2((sn, sn))
    mask1 = (c1 // CHUNK) <= ((r1 + past_len) // CHUNK)
    mask2 = ((c2 + P) // CHUNK) <= ((r2 + past_len) // CHUNK)
    v1 = cv_ref[...].astype(BF16)
    v2 = vn_ref[...]
    outs = []
    for mm in range(2):
        cols = slice(mm * HEAD_DIM, (mm + 1) * HEAD_DIM)
        q = q_ref[:, cols]
        s1 = jnp.where(mask1, _dot_nt(q, ck_ref[:, cols].astype(BF16)), NEG)
        s2 = jnp.where(mask2, _dot_nt(q, kn_ref[:, cols]), NEG)
        outs.append(_softmax_two(s1, s2, v1, v2, jnp.exp2))
    lam = _diff_lambda(lam_ref, lam_init)
    o_ref[...] = _diff_finish(outs[0], outs[1], lam, subg_ref[...], lam_init)


def _attn_b_sample_kernel(b1_ref, b2_ref, q_ref, kn_ref, vn_ref, ck_ref, cv_ref, o_ref):
    q = q_ref[...]
    s1 = _dot_nt(q, ck_ref[...].astype(BF16)) * SCALE + b1_ref[...]
    s2 = _dot_nt(q, kn_ref[...]) * SCALE + b2_ref[...]
    o_ref[...] = _softmax_two(s1, s2, cv_ref[...].astype(BF16), vn_ref[...]).astype(BF16)


def _attn_c_sample_kernel(trin_ref, tri_ref, q_ref, kn_ref, vn_ref, ck_ref, cv_ref, o_ref):
    sn = q_ref.shape[0]
    P = ck_ref.shape[0]
    tk = STICK_K_TILE
    q = q_ref[...]
    ri, ci = _iota2((sn, sn))
    acc, r = _stick_tile(_dot_nt(q, kn_ref[...]), vn_ref[...], trin_ref[...], jnp.zeros((sn, 1), F32), ci < ri)
    tri = tri_ref[...]
    for kb in reversed(range(P // tk)):
        rows = slice(kb * tk, (kb + 1) * tk)
        pv, r = _stick_tile(_dot_nt(q, ck_ref[rows, :].astype(BF16)), cv_ref[rows, :].astype(BF16), tri, r, None)
        acc = acc + pv
    o_ref[...] = acc.astype(BF16)


def _attn_d_sample_kernel(triu_ref, triun_ref, q_ref, kn_ref, vn_ref, ck_ref, cv_ref, lfp_ref, lfn_ref, o_ref):
    sn = q_ref.shape[0]
    q = q_ref[...]
    c_past = _dot3(jnp.broadcast_to(lfp_ref[...], (8, lfp_ref.shape[1])), triu_ref[...])[0:1, :]
    total = c_past[:, -1:]
    c_new = _dot3(jnp.broadcast_to(lfn_ref[...], (8, sn)), triun_ref[...])[0:1, :] + total
    ri, ci = _iota2((sn, sn))
    cq = jnp.sum(jnp.where(ri == ci, jnp.broadcast_to(c_new, (sn, sn)), 0.0), axis=-1, keepdims=True)
    s1 = _dot_nt(q, ck_ref[...].astype(BF16)) + (cq - c_past) * LOG2E
    s2 = jnp.where(ci <= ri, _dot_nt(q, kn_ref[...]) + (cq - c_new) * LOG2E, NEG)
    o_ref[...] = _softmax_two(s1, s2, cv_ref[...].astype(BF16), vn_ref[...], jnp.exp2).astype(BF16)


def _attn_sample(kern, consts, q, kn, vn, ck, cv, extra, nb, sn, P, nh, name):
    w = D_MODEL // nh
    bh = lambda b, h: (b, h)
    in_specs = [_const_spec(c.shape) for c in consts]
    in_specs += [pl.BlockSpec((sn, w), bh)] * 3 + [pl.BlockSpec((P, w), bh)] * 2
    in_specs += [pl.BlockSpec((None, 1, n), lambda b, h: (b * nh + h, 0, 0)) for _, n in extra]
    return pl.pallas_call(
        kern, grid=(nb, nh), in_specs=in_specs,
        out_specs=pl.BlockSpec((sn, w), bh),
        out_shape=jax.ShapeDtypeStruct((nb * sn, D_MODEL), BF16),
        compiler_params=_params("arbitrary", "arbitrary"), name=name)(
            *consts, q, kn, vn, ck, cv, *[a for a, _ in extra])


def _rope_tables(pos):
    inv = jnp.exp(-math.log(ROPE_THETA) * jnp.arange(ROPE_HALF, dtype=F32) * (2.0 / ROPE_DIM))
    ang = pos.astype(F32)[:, None] * inv[None, :]
    cos, sin = jnp.cos(ang), jnp.sin(ang)
    n = pos.shape[0]
    pad = jnp.zeros((n, HEAD_DIM - ROPE_DIM), F32)
    zero = jnp.zeros((n, ROPE_HALF), F32)
    c = jnp.concatenate([cos, cos, pad + 1.0], axis=1)
    sa = jnp.concatenate([-sin, zero, pad], axis=1)
    sb = jnp.concatenate([zero, sin, pad], axis=1)
    return c, sa, sb


def _band_bias(rel_bias, qpos, kpos):
    rel = jnp.clip(qpos[:, None] - kpos[None, :], -REL_CLIP, REL_CLIP) + REL_CLIP
    qc, kc = qpos[:, None] // CHUNK, kpos[None, :] // CHUNK
    mask = (kc <= qc) & (kc >= qc - BAND_CHUNKS)
    return jnp.where(mask[None], rel_bias.astype(F32)[:, rel], NEG)


def _band_bias_tile(rel_bias):
    qpos = BAND_PAST + jnp.arange(BAND_ROWS, dtype=jnp.int32)
    kpos = jnp.arange(BAND_WINDOW, dtype=jnp.int32)
    rel = jnp.clip(qpos[:, None] - kpos[None, :], -REL_CLIP, REL_CLIP) + REL_CLIP
    onehot = (rel[:, :, None] == jnp.arange(2 * REL_CLIP + 1, dtype=jnp.int32)).astype(F32)
    bias = jnp.einsum("ajt,ht->haj", onehot, rel_bias.astype(F32), precision=lax.Precision.HIGHEST)
    qc, kc = qpos[:, None] // CHUNK, kpos[None, :] // CHUNK
    mask = (kc <= qc) & (kc >= qc - BAND_CHUNKS)
    return jnp.where(mask[None], bias, NEG)


def _tri(n, fn):
    i = jnp.arange(n)
    return fn(i[:, None], i[None, :]).astype(BF16)


def _rows_per_head(a, nb, n):
    return jnp.transpose(a[:, :N_HEADS].reshape(nb, n, N_HEADS), (0, 2, 1)).reshape(nb * N_HEADS, 1, n)


def kernel(x_prompt, x_sample, cache_a_k, cache_a_v, cache_b_k, cache_b_v, cache_c_k, cache_c_v, cache_d_k, cache_d_v, cache_d_logf, norm_g, w_up, w_down, a_w_in, a_lam_q1, a_lam_k1, a_lam_q2, a_lam_k2, a_sub_g, a_w_out, b_w_in, b_rel_bias, b_w_out, c_w_in, c_w_out, d_w_in, d_b_f, d_w_out):
    B, S, _ = x_prompt.shape
    NB, SN, _ = x_sample.shape
    P = cache_a_k.shape[2]
    PB = cache_b_k.shape[2]
    depth = norm_g.shape[0]
    H = N_HEADS
    tm = TOKEN_TILE
    nblk = S // tm
    assert NB * SN == tm and S % Q_TILE == 0 and PB == BAND_PAST

    xp = x_prompt.reshape(B * S, D_MODEL)
    xs = x_sample.reshape(NB * SN, D_MODEL)
    pos_p = jnp.arange(S, dtype=jnp.int32)
    pos_s = P + jnp.arange(SN, dtype=jnp.int32)

    flat = lambda c: c.reshape(c.shape[0] * c.shape[1], D_MODEL)
    tri_strict = _tri(STICK_K_TILE, lambda r, c: r > c)
    st = [([], []) for _ in range(N_MIXERS)]

    for i in range(depth):
        m, r = i % N_MIXERS, i // N_MIXERS
        g = norm_g[i].astype(F32)
        if m == 0:
            lam_init = 0.8 - 0.6 * math.exp(-0.3 * i)
            w = a_w_in[r].astype(BF16)
            lamvec = jnp.stack([a_lam_q1[r], a_lam_k1[r], a_lam_q2[r], a_lam_k2[r]]).astype(F32)
            subg = a_sub_g[r].astype(F32).reshape(1, DIFF_VDIM)
            qb, kf, kb, vf, vb = _proj_in(xp, g[0], w, nblk=nblk, rope=_rope_tables(pos_p), scale_q=SCALE_LOG2E)
            op = _attn_a_prompt(qb, kb, vb, lamvec, subg, B, S, lam_init)
            sp = (kf.reshape(B, S, H, HEAD_DIM), vf.reshape(B, S, DIFF_HEADS, DIFF_VDIM))
            rope_s = tuple(jnp.tile(t, (NB, 1)) for t in _rope_tables(pos_s))
            qb, kf, kb, vf, vb = _proj_in(xs, g[0], w, nblk=1, rope=rope_s, scale_q=SCALE_LOG2E)
            kern = functools.partial(_attn_a_sample_kernel, lam_init=lam_init, past_len=P)
            os_ = _attn_sample(kern, [lamvec, subg], qb, kb, vb, flat(cache_a_k[r]), flat(cache_a_v[r]), [],
                               NB, SN, P, DIFF_HEADS, "attn_a_sample")
            ss = (kf.reshape(NB, SN, H, HEAD_DIM), vf.reshape(NB, SN, DIFF_HEADS, DIFF_VDIM))
            w_out = a_w_out[r]
        elif m == 1:
            w = b_w_in[r].astype(BF16)
            qb, kb, vb, kl, vl = _proj_in(xp, g[0], w, nblk=nblk, kv_last=True)
            op = _attn_b_prompt(qb, kb, vb, _band_bias_tile(b_rel_bias[r]), B, S)
            sp = (kl.reshape(B, PB, H, HEAD_DIM), vl.reshape(B, PB, H, HEAD_DIM))
            qb, kb, vb, kl, vl = _proj_in(xs, g[0], w, nblk=1, kv_last=True)
            kpos = P - PB + jnp.arange(PB + SN, dtype=jnp.int32)
            kvis = _band_bias(b_rel_bias[r], pos_s, kpos)
            kvis = jnp.where((kpos >= 0)[None, None, :], kvis, NEG)
            b1 = kvis[:, :, :PB].reshape(H * SN, PB)
            b2 = kvis[:, :, PB:].reshape(H * SN, SN)
            os_ = _attn_b_sample(qb, kb, vb, flat(cache_b_k[r]), flat(cache_b_v[r]), b1, b2, NB, SN, PB)
            kn = kl.reshape(NB, SN, H, HEAD_DIM)
            vn = vl.reshape(NB, SN, H, HEAD_DIM)
            ss = (jnp.concatenate([cache_b_k[r], kn], axis=1)[:, SN:], jnp.concatenate([cache_b_v[r], vn], axis=1)[:, SN:])
            w_out = b_w_out[r]
        elif m == 2:
            w = c_w_in[r].astype(BF16)
            qb, kf, kb, vf, vb = _proj_in(xp, g[0], w, nblk=nblk, scale_q=SCALE)
            op = _attn_c_prompt(qb, kb, vb, tri_strict, B, S)
            sp = (kf.reshape(B, S, H, HEAD_DIM), vf.reshape(B, S, H, HEAD_DIM))
            qb, kf, kb, vf, vb = _proj_in(xs, g[0], w, nblk=1, scale_q=SCALE)
            os_ = _attn_sample(_attn_c_sample_kernel, [_tri(SN, lambda r_, c_: r_ > c_), tri_strict], qb, kb, vb,
                               flat(cache_c_k[r]), flat(cache_c_v[r]), [], NB, SN, P, H, "attn_c_sample")
            ss = (kf.reshape(NB, SN, H, HEAD_DIM), vf.reshape(NB, SN, H, HEAD_DIM))
            w_out = c_w_out[r]
        else:
            w_all = d_w_in[r]
            w = w_all[:, :3 * D_MODEL].astype(BF16)
            wf = jnp.pad(w_all[:, 3 * D_MODEL:], ((0, 0), (0, LANES - H))).astype(BF16)
            bf = jnp.pad(d_b_f[r].astype(F32), (0, LANES - H)).reshape(1, LANES)
            head_of_col = (jnp.arange(D_MODEL)[:, None] // HEAD_DIM == jnp.arange(LANES)[None, :]).astype(BF16)
            fox = (wf, bf, _tri(tm, lambda r_, c_: r_ >= c_), head_of_col)
            qb, kf, kb, vf, vb, lf, c, norms = _proj_in(xp, g[0], w, nblk=nblk, scale_q=SCALE_LOG2E, fox=fox)
            ck = _rows_per_head(c, B, S).reshape(B * H // FLASH_HEADS, FLASH_HEADS, S)
            op = _attn_d_prompt(qb, kb, vb, ck[..., None], ck, _first_live_tile(norms, c, B, S), B, S)
            sp = (kf.reshape(B, S, H, HEAD_DIM), vf.reshape(B, S, H, HEAD_DIM), lf[:, :H].reshape(B, S, H))
            qb, kf, kb, vf, vb, lf, c, _ = _proj_in(xs, g[0], w, nblk=1, scale_q=SCALE_LOG2E, fox=fox)
            lfp = jnp.transpose(cache_d_logf[r].astype(F32), (0, 2, 1)).reshape(NB * H, 1, P)
            lfn = _rows_per_head(lf, NB, SN)
            triu = [_tri(P, lambda r_, c_: r_ <= c_), _tri(SN, lambda r_, c_: r_ <= c_)]
            os_ = _attn_sample(_attn_d_sample_kernel, triu, qb, kb, vb, flat(cache_d_k[r]), flat(cache_d_v[r]),
                               [(lfp, P), (lfn, SN)], NB, SN, P, H, "attn_d_sample")
            ss = (kf.reshape(NB, SN, H, HEAD_DIM), vf.reshape(NB, SN, H, HEAD_DIM), lf[:, :H].reshape(NB, SN, H))
            w_out = d_w_out[r]
        st[m][0].append(sp)
        st[m][1].append(ss)
        wo, wu, wd = w_out.astype(BF16), w_up[i].astype(BF16), w_down[i].astype(BF16)
        xp = _out_mlp(op, xp, g[1:4], wo, wu, wd)
        xs = _out_mlp(os_, xs, g[1:4], wo, wu, wd)

    stack = lambda states, j: jnp.stack([s[j] for s in states], axis=0)
    pa, sa = st[0]
    pb, sb = st[1]
    pc, sc = st[2]
    pd, sd = st[3]
    return (xp.reshape(B, S, D_MODEL), xs.reshape(NB, SN, D_MODEL),
            stack(pa, 0), stack(pa, 1), stack(sa, 0), stack(sa, 1),
            stack(pb, 0), stack(pb, 1), stack(sb, 0), stack(sb, 1),
            stack(pc, 0), stack(pc, 1), stack(sc, 0), stack(sc, 1),
            stack(pd, 0), stack(pd, 1), stack(pd, 2), stack(sd, 0), stack(sd, 1), stack(sd, 2))


def _attn_b_sample(q, kn, vn, ck, cv, b1, b2, nb, sn, pb):
    d = HEAD_DIM
    bh = lambda b, h: (b, h)
    return pl.pallas_call(
        _attn_b_sample_kernel, grid=(nb, N_HEADS),
        in_specs=[pl.BlockSpec((sn, pb), lambda b, h: (h, 0)), pl.BlockSpec((sn, sn), lambda b, h: (h, 0)),
                  pl.BlockSpec((sn, d), bh), pl.BlockSpec((sn, d), bh), pl.BlockSpec((sn, d), bh),
                  pl.BlockSpec((pb, d), bh), pl.BlockSpec((pb, d), bh)],
        out_specs=pl.BlockSpec((sn, d), bh),
        out_shape=jax.ShapeDtypeStruct((nb * sn, D_MODEL), BF16),
        compiler_params=_params("arbitrary", "arbitrary"), name="attn_b_sample")(b1, b2, q, kn, vn, ck, cv)
```

```python
import functools
import math

import jax
import jax.numpy as jnp
from jax import lax
from jax.experimental import pallas as pl
from jax.experimental.pallas import tpu as pltpu

D_MODEL = 1024
HEAD_DIM = 128
N_HEADS = D_MODEL // HEAD_DIM
DIFF_HEADS = N_HEADS // 2
DIFF_VDIM = 2 * HEAD_DIM
N_MIXERS = 4
CHUNK = 64
ROPE_THETA = 500000.0
ROPE_DIM = HEAD_DIM // 4
ROPE_HALF = ROPE_DIM // 2
BAND_CHUNKS = 8
BAND_PAST = BAND_CHUNKS * CHUNK
REL_CLIP = 128
D_FF = 4 * D_MODEL
NORM_EPS = 1e-6
SCALE = HEAD_DIM ** -0.5
LOG2E = math.log2(math.e)
SCALE_LOG2E = SCALE * LOG2E
NEG = -1e30

LANES = 128
TOKEN_TILE = 512
Q_TILE = 512
FLASH_ROWS = 128
FLASH_HEADS = 2
BAND_ROWS = 128
BAND_WINDOW = BAND_PAST + BAND_ROWS
STICK_K_TILE = 256
STICK_UNDERFLOW = -104.0
FLASH_UNDERFLOW = -152.0
NORM_MARGIN = 1.02
VMEM_LIMIT_BYTES = 56 * 1024 * 1024

BF16 = jnp.bfloat16
F32 = jnp.float32


def _dot(a, b):
    return jnp.dot(a, b, preferred_element_type=F32)


def _dot_nt(a, b):
    return lax.dot_general(a, b, (((1,), (1,)), ((), ())), preferred_element_type=F32)


def _rms(xf, g):
    ms = jnp.mean(xf * xf, axis=-1, keepdims=True)
    return xf * lax.rsqrt(ms + NORM_EPS) * g


def _split3(x):
    hi = x.astype(BF16)
    r = x - hi.astype(F32)
    mid = r.astype(BF16)
    lo = (r - mid.astype(F32)).astype(BF16)
    return hi, mid, lo


def _dot3(x, t):
    hi, mid, lo = _split3(x)
    return _dot(hi, t) + _dot(mid, t) + _dot(lo, t)


def _dot3_t(t, x):
    hi, mid, lo = _split3(x)
    return _dot(t, hi) + _dot(t, mid) + _dot(t, lo)


def _dot2(x, t):
    hi = x.astype(BF16)
    lo = (x - hi.astype(F32)).astype(BF16)
    return _dot(hi, t) + _dot(lo, t)


def _log_sigmoid_pair(z):
    t = jnp.log(1.0 + jnp.exp(-jnp.abs(z)))
    return jnp.minimum(z, 0.0) - t, -jnp.maximum(z, 0.0) - t


def _params(*sem):
    return pltpu.CompilerParams(dimension_semantics=sem, vmem_limit_bytes=VMEM_LIMIT_BYTES)


def _const_spec(shape):
    nd = len(shape)
    return pl.BlockSpec(shape, lambda *_: (0,) * nd, pipeline_mode=pl.Buffered(1))


def _proj_in_kernel(*refs, rope, scale_q, kv_last, fox, nblk):
    it = iter(refs)
    x_ref, g_ref, w_ref = next(it), next(it), next(it)
    if rope:
        cos_ref, sa_ref, sb_ref = next(it), next(it), next(it)
    if fox:
        wf_ref, bf_ref, tril_ref, heads_ref = next(it), next(it), next(it), next(it)
    qb_ref = next(it)
    if kv_last:
        kb_ref, vb_ref, kf_ref, vf_ref = next(it), next(it), next(it), next(it)
    else:
        kf_ref, kb_ref, vf_ref, vb_ref = next(it), next(it), next(it), next(it)
    if fox:
        lf_ref, c_ref, norm_ref, carry_ref = next(it), next(it), next(it), next(it)

    i = pl.program_id(0)
    h = _rms(x_ref[...], g_ref[...]).astype(BF16)
    cw = 4 * HEAD_DIM
    is_last = (i % nblk) == (nblk - 1)

    def rotate(xh):
        return (xh * cos_ref[...] + pltpu.roll(xh, HEAD_DIM - ROPE_HALF, 1) * sa_ref[...]
                + pltpu.roll(xh, ROPE_HALF, 1) * sb_ref[...])

    for part in range(3):
        for c in range(D_MODEL // cw):
            col = c * cw
            acc = _dot(h, w_ref[:, part * D_MODEL + col:part * D_MODEL + col + cw])
            if rope and part < 2:
                acc = jnp.concatenate(
                    [rotate(acc[:, hh * HEAD_DIM:(hh + 1) * HEAD_DIM]) for hh in range(cw // HEAD_DIM)], axis=1)
            if part == 0:
                if scale_q is not None:
                    acc = acc * scale_q
                qb_ref[:, col:col + cw] = acc.astype(BF16)
            else:
                fref, bref = (kf_ref, kb_ref) if part == 1 else (vf_ref, vb_ref)
                bref[:, col:col + cw] = acc.astype(BF16)
                if kv_last:
                    @pl.when(is_last)
                    def _():
                        fref[:, col:col + cw] = acc
                else:
                    fref[:, col:col + cw] = acc

    if fox:
        lf, _ = _log_sigmoid_pair(_dot(h, wf_ref[...]) + bf_ref[...])

        @pl.when((i % nblk) == 0)
        def _():
            carry_ref[...] = jnp.zeros_like(carry_ref)

        c_tile = _dot3_t(tril_ref[...], lf) + carry_ref[0:1, :]
        lf_ref[...] = lf
        c_ref[...] = c_tile
        carry_ref[...] = jnp.broadcast_to(c_tile[-1:, :], carry_ref.shape)

        def max_sq_norm(ref):
            x = ref[...].astype(F32)
            sq = _dot((x * x).astype(BF16), heads_ref[...])
            return jnp.broadcast_to(jnp.max(sq, axis=0, keepdims=True), (8, LANES))

        norm_ref[0:8, :] = max_sq_norm(qb_ref)
        norm_ref[8:16, :] = max_sq_norm(kb_ref)


def _proj_in(x, g, w, *, nblk, rope=None, scale_q=None, kv_last=False, fox=None):
    T = x.shape[0]
    tm = TOKEN_TILE
    assert T % tm == 0 and (T // tm) % nblk == 0
    n = T // tm
    row = lambda i: (i, 0)
    in_specs = [pl.BlockSpec((tm, D_MODEL), row), _const_spec((1, D_MODEL)), _const_spec((D_MODEL, 3 * D_MODEL))]
    args = [x, g.reshape(1, D_MODEL), w]
    if rope is not None:
        tab = lambda i: (i % nblk, 0)
        in_specs += [pl.BlockSpec((tm, HEAD_DIM), tab)] * 3
        args += list(rope)
    if fox is not None:
        in_specs += [_const_spec((D_MODEL, LANES)), _const_spec((1, LANES)), _const_spec((tm, tm)),
                     _const_spec((D_MODEL, LANES))]
        args += list(fox)
    full_b = jax.ShapeDtypeStruct((T, D_MODEL), BF16)
    full_f = jax.ShapeDtypeStruct((T, D_MODEL), F32)
    tile = pl.BlockSpec((tm, D_MODEL), row)
    if kv_last:
        last_f = jax.ShapeDtypeStruct((T // nblk, D_MODEL), F32)
        last = pl.BlockSpec((tm, D_MODEL), lambda i: (i // nblk, 0))
        out_shape = [full_b, full_b, full_b, last_f, last_f]
        out_specs = [tile, tile, tile, last, last]
    else:
        out_shape = [full_b, full_f, full_b, full_f, full_b]
        out_specs = [tile] * 5
    scratch = []
    if fox is not None:
        out_shape += [jax.ShapeDtypeStruct((T, LANES), F32)] * 2 + [jax.ShapeDtypeStruct((n * 16, LANES), F32)]
        out_specs += [pl.BlockSpec((tm, LANES), row)] * 2 + [pl.BlockSpec((16, LANES), row)]
        scratch = [pltpu.VMEM((8, LANES), F32)]
    kern = functools.partial(_proj_in_kernel, rope=rope is not None, scale_q=scale_q, kv_last=kv_last,
                             fox=fox is not None, nblk=nblk)
    return pl.pallas_call(
        kern, grid=(n,), in_specs=in_specs, out_specs=out_specs, out_shape=out_shape,
        scratch_shapes=scratch, compiler_params=_params("arbitrary"), name="proj_in")(*args)


def _out_mlp_kernel(o_ref, x_ref, g_ref, wo_ref, wu_ref, wd_ref, y_ref):
    a = _dot(o_ref[...], wo_ref[...])
    x1 = x_ref[...] + _rms(a, g_ref[0:1, :])
    h2 = _rms(x1, g_ref[1:2, :]).astype(BF16)
    acc = jnp.zeros_like(x1)
    for c in range(D_FF // D_MODEL):
        u = jnp.maximum(_dot(h2, wu_ref[:, c * D_MODEL:(c + 1) * D_MODEL]), 0.0)
        acc = acc + _dot((u * u).astype(BF16), wd_ref[c * D_MODEL:(c + 1) * D_MODEL, :])
    y_ref[...] = x1 + _rms(acc, g_ref[2:3, :])


def _out_mlp(o, x, g3, wo, wu, wd):
    T = x.shape[0]
    tm = TOKEN_TILE
    row = lambda i: (i, 0)
    return pl.pallas_call(
        _out_mlp_kernel, grid=(T // tm,),
        in_specs=[pl.BlockSpec((tm, D_MODEL), row), pl.BlockSpec((tm, D_MODEL), row), _const_spec((3, D_MODEL)),
                  _const_spec((D_MODEL, D_MODEL)), _const_spec((D_MODEL, D_FF)), _const_spec((D_FF, D_MODEL))],
        out_specs=pl.BlockSpec((tm, D_MODEL), row),
        out_shape=jax.ShapeDtypeStruct((T, D_MODEL), F32),
        compiler_params=_params("arbitrary"), name="out_mlp")(o, x, g3, wo, wu, wd)


def _lane_rep(x, width):
    return x if width == LANES else jnp.concatenate([x] * (width // LANES), axis=1)


def _iota2(shape):
    return lax.broadcasted_iota(jnp.int32, shape, 0), lax.broadcasted_iota(jnp.int32, shape, 1)


def _causal_flash(first, last, qk_fns, v_fns, diag_mask_fn, shift_ref, s_ref, p_ref, alpha_ref, m_ref, l_ref,
                  acc_ref):
    _, n, tq, tk = s_ref.shape
    rb = FLASH_ROWS
    width = acc_ref.shape[2]

    def weigh_values(slot, t):
        for x in range(n):
            acc_ref[x] = _lane_rep(alpha_ref[slot, x], width) * acc_ref[x] + _dot(p_ref[slot, x], v_fns[x](t))

    def softmax_tile(slot, diagonal):
        for x in range(n):
            for r in range(tq // rb):
                rows = slice(r * rb, (r + 1) * rb)
                s = s_ref[slot, x, rows, :]
                if diagonal:
                    s = jnp.where(diag_mask_fn(r * rb, (rb, tk)), s, NEG)
                m_prev = m_ref[x, rows, :]
                m_cur = jnp.broadcast_to(jnp.max(s, axis=-1, keepdims=True), (rb, LANES))
                if shift_ref is not None:
                    m_cur = m_cur + shift_ref[x, rows, :]
                m_new = jnp.maximum(m_prev, m_cur)
                alpha = jnp.exp2(m_prev - m_new)
                m_sub = m_new if shift_ref is None else m_new - shift_ref[x, rows, :]
                p = jnp.exp2(s - _lane_rep(m_sub, tk))
                l_ref[x, rows, :] = (alpha * l_ref[x, rows, :]
                                     + jnp.broadcast_to(jnp.sum(p, axis=-1, keepdims=True), (rb, LANES)))
                m_ref[x, rows, :] = m_new
                alpha_ref[slot, x, rows, :] = alpha
                p_ref[slot, x, rows, :] = p.astype(BF16)

    def scores(slot, t):
        for x in range(n):
            s_ref[slot, x] = qk_fns[x](t)

    def step(j, slot):
        weigh_values(1 - slot, jnp.maximum(j - 1, first))
        softmax_tile(slot, False)
        scores(1 - slot, j + 1)

    m_ref[...] = jnp.full_like(m_ref, NEG)
    l_ref[...] = jnp.zeros_like(l_ref)
    acc_ref[...] = jnp.zeros_like(acc_ref)
    p_ref[...] = jnp.zeros_like(p_ref)
    alpha_ref[...] = jnp.ones_like(alpha_ref)

    n_full = last - first
    odd = n_full % 2

    @pl.when(odd == 1)
    def _():
        scores(1, first)
        step(first, 1)

    @pl.when(odd == 0)
    def _():
        scores(0, first)

    def body(jj, carry):
        j = first + odd + 2 * jj
        step(j, 0)
        step(j + 1, 1)
        return carry

    lax.fori_loop(0, n_full // 2, body, 0)
    weigh_values(1, jnp.maximum(last - 1, first))
    softmax_tile(0, True)
    weigh_values(0, last)


def _flash_scratch(n, tq, width):
    return [pltpu.VMEM((2, n, tq, tq), F32), pltpu.VMEM((2, n, tq, tq), BF16), pltpu.VMEM((2, n, tq, LANES), F32),
            pltpu.VMEM((n, tq, LANES), F32), pltpu.VMEM((n, tq, LANES), F32), pltpu.VMEM((n, tq, width), F32)]


def _diff_lambda(lam_ref, lam_init):
    lv = lam_ref[...]
    s1 = jnp.sum(lv[0:1, :] * lv[1:2, :], axis=-1, keepdims=True)
    s2 = jnp.sum(lv[2:3, :] * lv[3:4, :], axis=-1, keepdims=True)
    return jnp.exp(s1) - jnp.exp(s2) + lam_init


def _diff_finish(o0, o1, lam, subg, lam_init):
    o = o0 - lam * o1
    return (_rms(o, subg) * (1.0 - lam_init)).astype(BF16)


def _stick_tile(z, vc, tri, r_prev, causal):
    lsz, lk = _log_sigmoid_pair(z)
    if causal is not None:
        lk = jnp.where(causal, lk, 0.0)
    after = _dot2(lk, tri) + r_prev
    a = jnp.exp(lsz + after)
    if causal is not None:
        a = jnp.where(causal, a, 0.0)
    return _dot(a.astype(BF16), vc), r_prev + jnp.sum(lk, axis=-1, keepdims=True)


def _attn_a_prompt_kernel(lam_ref, subg_ref, q_ref, k_ref, v_ref, o_ref, s_ref, p_ref, alpha_ref, m_ref, l_ref,
                          acc_ref, *, lam_init):
    tq = q_ref.shape[0]
    i = pl.program_id(2)

    def rows_of(t):
        return pl.ds(pl.multiple_of(t * tq, tq), tq)

    def chunk_causal(row0, shape):
        ri, ci = _iota2(shape)
        return (ci // CHUNK) <= ((ri + row0) // CHUNK)

    def scores_fn(mm):
        cols = slice(mm * HEAD_DIM, (mm + 1) * HEAD_DIM)
        return lambda t: _dot_nt(q_ref[:, cols], k_ref[rows_of(t), cols])

    values_fn = lambda t: v_ref[rows_of(t), :]
    _causal_flash(0, i, [scores_fn(0), scores_fn(1)], [values_fn, values_fn], chunk_causal, None,
                  s_ref, p_ref, alpha_ref, m_ref, l_ref, acc_ref)
    outs = [acc_ref[mm] / _lane_rep(l_ref[mm], acc_ref.shape[2]) for mm in range(2)]
    lam = _diff_lambda(lam_ref, lam_init)
    o_ref[...] = _diff_finish(outs[0], outs[1], lam, subg_ref[...], lam_init)


def _attn_a_prompt(q, k, v, lamvec, subg, B, S, lam_init):
    tq = Q_TILE
    nq = S // tq
    w = DIFF_VDIM
    kern = functools.partial(_attn_a_prompt_kernel, lam_init=lam_init)
    return pl.pallas_call(
        kern, grid=(B, DIFF_HEADS, nq),
        in_specs=[_const_spec((4, HEAD_DIM)), _const_spec((1, w)),
                  pl.BlockSpec((tq, w), lambda b, g, i: (b * nq + i, g)),
                  pl.BlockSpec((S, w), lambda b, g, i: (b, g)),
                  pl.BlockSpec((S, w), lambda b, g, i: (b, g))],
        out_specs=pl.BlockSpec((tq, w), lambda b, g, i: (b * nq + i, g)),
        out_shape=jax.ShapeDtypeStruct((B * S, D_MODEL), BF16),
        scratch_shapes=_flash_scratch(2, tq, w),
        compiler_params=_params("arbitrary", "arbitrary", "arbitrary"), name="attn_a_prompt")(lamvec, subg, q, k, v)


def _attn_b_prompt_kernel(bias_ref, q_ref, k_ref, v_ref, o_ref, s_ref, p_ref):
    tq = q_ref.shape[0]
    rb, nwin = bias_ref.shape
    nr = tq // rb
    i = pl.program_id(2)

    def row_blocks(windows):
        for r, (start, nk) in enumerate(windows):
            rows = slice(r * rb, (r + 1) * rb)
            s_ref[r, :, :nk] = (_dot_nt(q_ref[rows, :], k_ref[pl.ds(start, nk), :]) * SCALE
                                + bias_ref[:, nwin - nk:])
        sums = []
        for r, (start, nk) in enumerate(windows):
            s = s_ref[r, :, :nk]
            p = jnp.exp(s - jnp.max(s, axis=-1, keepdims=True))
            sums.append(jnp.sum(p, axis=-1, keepdims=True))
            p_ref[r, :, :nk] = p.astype(BF16)
        for r, (start, nk) in enumerate(windows):
            rows = slice(r * rb, (r + 1) * rb)
            o_ref[rows, :] = (_dot(p_ref[r, :, :nk], v_ref[pl.ds(start, nk), :]) / sums[r]).astype(BF16)

    @pl.when(i > 0)
    def _():
        row_blocks([(pl.multiple_of(i * tq + (r + 1) * rb - nwin, rb), nwin) for r in range(nr)])

    @pl.when(i == 0)
    def _():
        row_blocks([(0, (r + 1) * rb) for r in range(nr)])


def _attn_b_prompt(q, k, v, bias, B, S):
    tq = Q_TILE
    assert tq + BAND_ROWS >= BAND_WINDOW
    nq = S // tq
    d = HEAD_DIM
    return pl.pallas_call(
        _attn_b_prompt_kernel, grid=(B, N_HEADS, nq),
        in_specs=[pl.BlockSpec((None, BAND_ROWS, BAND_WINDOW), lambda b, h, i: (h, 0, 0)),
                  pl.BlockSpec((tq, d), lambda b, h, i: (b * nq + i, h)),
                  pl.BlockSpec((S, d), lambda b, h, i: (b, h)),
                  pl.BlockSpec((S, d), lambda b, h, i: (b, h))],
        out_specs=pl.BlockSpec((tq, d), lambda b, h, i: (b * nq + i, h)),
        out_shape=jax.ShapeDtypeStruct((B * S, D_MODEL), BF16),
        scratch_shapes=[pltpu.VMEM((tq // BAND_ROWS, BAND_ROWS, BAND_WINDOW), F32),
                        pltpu.VMEM((tq // BAND_ROWS, BAND_ROWS, BAND_WINDOW), BF16)],
        compiler_params=_params("arbitrary", "arbitrary", "arbitrary"), name="attn_b_prompt")(bias, q, k, v)


def _attn_c_prompt_kernel(tri_ref, q_ref, k_ref, v_ref, o_ref, r_ref, acc_ref):
    tq = q_ref.shape[0]
    tk = STICK_K_TILE
    assert tq == 2 * tk
    i = pl.program_id(2)
    q = q_ref[...]
    tri = tri_ref[...]
    lo, hi = slice(0, tk), slice(tk, tq)
    ri, ci = _iota2((tk, tk))
    below_diag = ci < ri

    def keys(kb):
        return pl.ds(pl.multiple_of(kb * tk, tk), tk)

    def parts(qrows, krows, causal, valid=None):
        lsz, lk = _log_sigmoid_pair(_dot_nt(qrows, k_ref[krows, :]))
        if causal is not None:
            lk = jnp.where(causal, lk, 0.0)
            lsz = jnp.where(causal, lsz, NEG)
        if valid is not None:
            lk = jnp.where(valid, lk, 0.0)
            lsz = jnp.where(valid, lsz, NEG)
        return lsz, _dot2(lk, tri), jnp.sum(lk, axis=-1, keepdims=True)

    def weigh(lsz, within, r_right, krows):
        return _dot(jnp.exp(lsz + within + r_right).astype(BF16), v_ref[krows, :])

    k_new, k_mid, k_old = keys(2 * i + 1), keys(2 * i), keys(jnp.maximum(2 * i - 1, 0))
    lsz_a, within_a, tot_a = parts(q[hi], k_new, below_diag)
    lsz_b0, within_b0, tot_b0 = parts(q[lo], k_mid, below_diag)
    lsz_b1, within_b1, tot_b1 = parts(q[hi], k_mid, None)
    lsz_c, within_c, tot_c = parts(q, k_old, None, valid=i > 0)
    zero = jnp.zeros((tk, 1), F32)
    r_b = jnp.concatenate([zero, tot_a], axis=0)
    r_c = r_b + jnp.concatenate([tot_b0, tot_b1], axis=0)
    acc_ref[lo, :] = weigh(lsz_b0, within_b0, zero, k_mid) + weigh(lsz_c[lo], within_c[lo], r_c[lo], k_old)
    acc_ref[hi, :] = (weigh(lsz_a, within_a, zero, k_new) + weigh(lsz_b1, within_b1, tot_a, k_mid)
                      + weigh(lsz_c[hi], within_c[hi], r_c[hi], k_old))
    r_ref[...] = r_c + tot_c

    def cond(c):
        kb, rmax = c
        return jnp.logical_and(kb >= 0, rmax > STICK_UNDERFLOW)

    def body(c):
        kb, _ = c
        lsz, within, tot = parts(q, keys(kb), None)
        acc_ref[...] += weigh(lsz, within, r_ref[...], keys(kb))
        r_ref[...] += tot
        return kb - 1, jnp.max(r_ref[...])

    lax.while_loop(cond, body, (2 * i - 2, jnp.max(r_ref[...])))
    o_ref[...] = acc_ref[...].astype(BF16)


def _attn_c_prompt(q, k, v, tri, B, S):
    tq = Q_TILE
    nq = S // tq
    d = HEAD_DIM
    return pl.pallas_call(
        _attn_c_prompt_kernel, grid=(B, N_HEADS, nq),
        in_specs=[_const_spec((STICK_K_TILE, STICK_K_TILE)),
                  pl.BlockSpec((tq, d), lambda b, h, i: (b * nq + i, h)),
                  pl.BlockSpec((S, d), lambda b, h, i: (b, h)),
                  pl.BlockSpec((S, d), lambda b, h, i: (b, h))],
        out_specs=pl.BlockSpec((tq, d), lambda b, h, i: (b * nq + i, h)),
        out_shape=jax.ShapeDtypeStruct((B * S, D_MODEL), BF16),
        scratch_shapes=[pltpu.VMEM((tq, 1), F32), pltpu.VMEM((tq, d), F32)],
        compiler_params=_params("arbitrary", "arbitrary", "arbitrary"), name="attn_c_prompt")(tri, q, k, v)


def _attn_d_prompt_kernel(first_ref, q_ref, k_ref, v_ref, cq_ref, ck_ref, o_ref, s_ref, p_ref, alpha_ref, m_ref,
                          l_ref, acc_ref, cqb_ref):
    tq = q_ref.shape[0]
    nh = cqb_ref.shape[0]
    i = pl.program_id(2)
    first = first_ref[(pl.program_id(0) * pl.num_programs(1) + pl.program_id(1)) * pl.num_programs(2) + i]
    lane = lax.broadcasted_iota(jnp.int32, cq_ref.shape, 1)
    for x in range(nh):
        col = jnp.sum(jnp.where(lane == pl.program_id(1) * nh + x, cq_ref[...], 0.0), axis=-1, keepdims=True)
        cqb_ref[x] = jnp.broadcast_to(col * LOG2E, cqb_ref.shape[1:])

    def rows_of(t):
        return pl.ds(pl.multiple_of(t * tq, tq), tq)

    def causal(row0, shape):
        ri, ci = _iota2(shape)
        return ci <= ri + row0

    def scores_fn(x):
        cols = slice(x * HEAD_DIM, (x + 1) * HEAD_DIM)
        return lambda t: _dot_nt(q_ref[:, cols], k_ref[rows_of(t), cols]) - ck_ref[x:x + 1, rows_of(t)] * LOG2E

    def values_fn(x):
        cols = slice(x * HEAD_DIM, (x + 1) * HEAD_DIM)
        return lambda t: v_ref[rows_of(t), cols]

    _causal_flash(first, i, [scores_fn(x) for x in range(nh)], [values_fn(x) for x in range(nh)], causal, cqb_ref,
                  s_ref, p_ref, alpha_ref, m_ref, l_ref, acc_ref)
    for x in range(nh):
        o_ref[:, x * HEAD_DIM:(x + 1) * HEAD_DIM] = (acc_ref[x] / l_ref[x]).astype(BF16)


def _first_live_tile(norms, c, B, S):
    tq = Q_TILE
    nq = S // tq
    H = N_HEADS
    nr = jnp.sqrt(norms.reshape(B, nq, 2, 8, LANES)[:, :, :, 0, :H]) * NORM_MARGIN
    qn, kn = nr[:, :, 0], nr[:, :, 1]
    ct = c[:, :H].reshape(B, nq, tq, H) * LOG2E
    cmax, cmin = jnp.max(ct, axis=2), jnp.min(ct, axis=2)
    bound = (qn[:, :, None] * (kn[:, None, :] + kn[:, :, None])
             + cmax[:, :, None] - cmin[:, None, :])
    t = jnp.arange(nq)
    live = (bound >= FLASH_UNDERFLOW) & (t[None, None, :, None] <= t[None, :, None, None])
    live = live.reshape(B, nq, nq, H // FLASH_HEADS, FLASH_HEADS).any(axis=-1)
    first = jnp.argmax(live, axis=2).astype(jnp.int32)
    return jnp.transpose(first, (0, 2, 1)).reshape(-1)


def _attn_d_prompt(q, k, v, cq, ck, first, B, S):
    tq = Q_TILE
    nq = S // tq
    nh = FLASH_HEADS
    w = nh * HEAD_DIM
    G = N_HEADS // nh
    return pl.pallas_call(
        _attn_d_prompt_kernel, grid=(B, G, nq),
        in_specs=[pl.BlockSpec(memory_space=pltpu.SMEM),
                  pl.BlockSpec((tq, w), lambda b, g, i: (b * nq + i, g)),
                  pl.BlockSpec((S, w), lambda b, g, i: (b, g)),
                  pl.BlockSpec((S, w), lambda b, g, i: (b, g)),
                  pl.BlockSpec((tq, LANES), lambda b, g, i: (b * nq + i, 0)),
                  pl.BlockSpec((None, nh, S), lambda b, g, i: (b * G + g, 0, 0))],
        out_specs=pl.BlockSpec((tq, w), lambda b, g, i: (b * nq + i, g)),
        out_shape=jax.ShapeDtypeStruct((B * S, D_MODEL), BF16),
        scratch_shapes=_flash_scratch(nh, tq, HEAD_DIM) + [pltpu.VMEM((nh, tq, LANES), F32)],
        compiler_params=_params("arbitrary", "arbitrary", "arbitrary"), name="attn_d_prompt")(
            first, q, k, v, cq, ck)


def _softmax_two(s1, s2, v1, v2, exp=jnp.exp):
    m = jnp.maximum(jnp.max(s1, axis=-1, keepdims=True), jnp.max(s2, axis=-1, keepdims=True))
    p1 = exp(s1 - m)
    p2 = exp(s2 - m)
    l = jnp.sum(p1, axis=-1, keepdims=True) + jnp.sum(p2, axis=-1, keepdims=True)
    return (_dot(p1.astype(BF16), v1) + _dot(p2.astype(BF16), v2)) / l


def _head_cols(h, width=HEAD_DIM):
    return slice(h * width, (h + 1) * width)


def _attn_a_sample_kernel(lam_ref, subg_ref, q_ref, kn_ref, vn_ref, ck_ref, cv_ref, o_ref, *, lam_init, past_len):
    sn = q_ref.shape[0]
    P = ck_ref.shape[0]
    r1, c1 = _iota2((sn, P))
    r2, c2 = _iota2((sn, sn))
    mask1 = (c1 // CHUNK) <= ((r1 + past_len) // CHUNK)
    mask2 = ((c2 + P) // CHUNK) <= ((r2 + past_len) // CHUNK)
    lam = _diff_lambda(lam_ref, lam_init)
    for g in range(DIFF_HEADS):
        gc = _head_cols(g, DIFF_VDIM)
        v1 = cv_ref[:, gc].astype(BF16)
        v2 = vn_ref[:, gc]
        outs = []
        for mm in range(2):
            cols = _head_cols(2 * g + mm)
            q = q_ref[:, cols]
            s1 = jnp.where(mask1, _dot_nt(q, ck_ref[:, cols].astype(BF16)), NEG)
            s2 = jnp.where(mask2, _dot_nt(q, kn_ref[:, cols]), NEG)
            outs.append(_softmax_two(s1, s2, v1, v2, jnp.exp2))
        o_ref[:, gc] = _diff_finish(outs[0], outs[1], lam, subg_ref[...], lam_init)


def _attn_b_sample_kernel(b1_ref, b2_ref, q_ref, kn_ref, vn_ref, ck_ref, cv_ref, o_ref):
    sn = q_ref.shape[0]
    for h in range(N_HEADS):
        cols = _head_cols(h)
        rows = slice(h * sn, (h + 1) * sn)
        q = q_ref[:, cols]
        s1 = _dot_nt(q, ck_ref[:, cols].astype(BF16)) * SCALE + b1_ref[rows, :]
        s2 = _dot_nt(q, kn_ref[:, cols]) * SCALE + b2_ref[rows, :]
        o_ref[:, cols] = _softmax_two(s1, s2, cv_ref[:, cols].astype(BF16), vn_ref[:, cols]).astype(BF16)


def _attn_c_sample_kernel(trin_ref, tri_ref, q_ref, kn_ref, vn_ref, ck_ref, cv_ref, o_ref):
    sn = q_ref.shape[0]
    P = ck_ref.shape[0]
    tk = STICK_K_TILE
    ri, ci = _iota2((sn, sn))
    tri = tri_ref[...]
    for h in range(N_HEADS):
        cols = _head_cols(h)
        q = q_ref[:, cols]
        acc, r = _stick_tile(_dot_nt(q, kn_ref[:, cols]), vn_ref[:, cols], trin_ref[...], jnp.zeros((sn, 1), F32),
                             ci < ri)
        for kb in reversed(range(P // tk)):
            rows = slice(kb * tk, (kb + 1) * tk)
            pv, r = _stick_tile(_dot_nt(q, ck_ref[rows, cols].astype(BF16)), cv_ref[rows, cols].astype(BF16), tri,
                                r, None)
            acc = acc + pv
        o_ref[:, cols] = acc.astype(BF16)


def _attn_d_sample_kernel(triu_ref, triun_ref, q_ref, kn_ref, vn_ref, ck_ref, cv_ref, lfp_ref, lfn_ref, o_ref):
    sn = q_ref.shape[0]
    c_past_all = _dot3(lfp_ref[...], triu_ref[...])
    c_new_all = _dot3(lfn_ref[...], triun_ref[...]) + c_past_all[:, -1:]
    ri, ci = _iota2((sn, sn))
    for h in range(N_HEADS):
        cols = _head_cols(h)
        q = q_ref[:, cols]
        c_past, c_new = c_past_all[h:h + 1, :], c_new_all[h:h + 1, :]
        cq = jnp.sum(jnp.where(ri == ci, jnp.broadcast_to(c_new, (sn, sn)), 0.0), axis=-1, keepdims=True)
        s1 = _dot_nt(q, ck_ref[:, cols].astype(BF16)) + (cq - c_past) * LOG2E
        s2 = jnp.where(ci <= ri, _dot_nt(q, kn_ref[:, cols]) + (cq - c_new) * LOG2E, NEG)
        o_ref[:, cols] = _softmax_two(s1, s2, cv_ref[:, cols].astype(BF16), vn_ref[:, cols], jnp.exp2).astype(BF16)


def _attn_sample(kern, consts, q, kn, vn, ck, cv, extra, nb, sn, P, name):
    row = lambda b: (b, 0)
    in_specs = [_const_spec(c.shape) for c in consts]
    in_specs += [pl.BlockSpec((sn, D_MODEL), row)] * 3 + [pl.BlockSpec((P, D_MODEL), row)] * 2
    in_specs += [pl.BlockSpec((None,) + a.shape[1:], lambda b: (b, 0, 0)) for a in extra]
    return pl.pallas_call(
        kern, grid=(nb,), in_specs=in_specs,
        out_specs=pl.BlockSpec((sn, D_MODEL), row),
        out_shape=jax.ShapeDtypeStruct((nb * sn, D_MODEL), BF16),
        compiler_params=_params("arbitrary"), name=name)(*consts, q, kn, vn, ck, cv, *extra)


def _rope_tables(pos):
    inv = jnp.exp(-math.log(ROPE_THETA) * jnp.arange(ROPE_HALF, dtype=F32) * (2.0 / ROPE_DIM))
    ang = pos.astype(F32)[:, None] * inv[None, :]
    cos, sin = jnp.cos(ang), jnp.sin(ang)
    n = pos.shape[0]
    pad = jnp.zeros((n, HEAD_DIM - ROPE_DIM), F32)
    zero = jnp.zeros((n, ROPE_HALF), F32)
    c = jnp.concatenate([cos, cos, pad + 1.0], axis=1)
    sa = jnp.concatenate([-sin, zero, pad], axis=1)
    sb = jnp.concatenate([zero, sin, pad], axis=1)
    return c, sa, sb


def _band_bias(rel_bias, qpos, kpos):
    rel = jnp.clip(qpos[:, None] - kpos[None, :], -REL_CLIP, REL_CLIP) + REL_CLIP
    qc, kc = qpos[:, None] // CHUNK, kpos[None, :] // CHUNK
    mask = (kc <= qc) & (kc >= qc - BAND_CHUNKS)
    return jnp.where(mask[None], rel_bias.astype(F32)[:, rel], NEG)


def _band_bias_tile(rel_bias):
    qpos = BAND_PAST + jnp.arange(BAND_ROWS, dtype=jnp.int32)
    kpos = jnp.arange(BAND_WINDOW, dtype=jnp.int32)
    rel = jnp.clip(qpos[:, None] - kpos[None, :], -REL_CLIP, REL_CLIP) + REL_CLIP
    onehot = (rel[:, :, None] == jnp.arange(2 * REL_CLIP + 1, dtype=jnp.int32)).astype(F32)
    bias = jnp.einsum("ajt,ht->haj", onehot, rel_bias.astype(F32), precision=lax.Precision.HIGHEST)
    qc, kc = qpos[:, None] // CHUNK, kpos[None, :] // CHUNK
    mask = (kc <= qc) & (kc >= qc - BAND_CHUNKS)
    return jnp.where(mask[None], bias, NEG)


def _tri(n, fn):
    i = jnp.arange(n)
    return fn(i[:, None], i[None, :]).astype(BF16)


def _rows_per_head(a, nb, n):
    return jnp.transpose(a[:, :N_HEADS].reshape(nb, n, N_HEADS), (0, 2, 1)).reshape(nb * N_HEADS, 1, n)


def kernel(x_prompt, x_sample, cache_a_k, cache_a_v, cache_b_k, cache_b_v, cache_c_k, cache_c_v, cache_d_k, cache_d_v, cache_d_logf, norm_g, w_up, w_down, a_w_in, a_lam_q1, a_lam_k1, a_lam_q2, a_lam_k2, a_sub_g, a_w_out, b_w_in, b_rel_bias, b_w_out, c_w_in, c_w_out, d_w_in, d_b_f, d_w_out):
    B, S, _ = x_prompt.shape
    NB, SN, _ = x_sample.shape
    P = cache_a_k.shape[2]
    PB = cache_b_k.shape[2]
    depth = norm_g.shape[0]
    H = N_HEADS
    tm = TOKEN_TILE
    nblk = S // tm
    assert NB * SN == tm and S % Q_TILE == 0 and PB == BAND_PAST

    xp = x_prompt.reshape(B * S, D_MODEL)
    xs = x_sample.reshape(NB * SN, D_MODEL)
    pos_p = jnp.arange(S, dtype=jnp.int32)
    pos_s = P + jnp.arange(SN, dtype=jnp.int32)

    flat = lambda c: c.reshape(c.shape[0] * c.shape[1], D_MODEL)
    tri_strict = _tri(STICK_K_TILE, lambda r, c: r > c)
    st = [([], []) for _ in range(N_MIXERS)]

    for i in range(depth):
        m, r = i % N_MIXERS, i // N_MIXERS
        g = norm_g[i].astype(F32)
        if m == 0:
            lam_init = 0.8 - 0.6 * math.exp(-0.3 * i)
            w = a_w_in[r].astype(BF16)
            lamvec = jnp.stack([a_lam_q1[r], a_lam_k1[r], a_lam_q2[r], a_lam_k2[r]]).astype(F32)
            subg = a_sub_g[r].astype(F32).reshape(1, DIFF_VDIM)
            qb, kf, kb, vf, vb = _proj_in(xp, g[0], w, nblk=nblk, rope=_rope_tables(pos_p), scale_q=SCALE_LOG2E)
            op = _attn_a_prompt(qb, kb, vb, lamvec, subg, B, S, lam_init)
            sp = (kf.reshape(B, S, H, HEAD_DIM), vf.reshape(B, S, DIFF_HEADS, DIFF_VDIM))
            rope_s = tuple(jnp.tile(t, (NB, 1)) for t in _rope_tables(pos_s))
            qb, kf, kb, vf, vb = _proj_in(xs, g[0], w, nblk=1, rope=rope_s, scale_q=SCALE_LOG2E)
            kern = functools.partial(_attn_a_sample_kernel, lam_init=lam_init, past_len=P)
            os_ = _attn_sample(kern, [lamvec, subg], qb, kb, vb, flat(cache_a_k[r]), flat(cache_a_v[r]), [],
                               NB, SN, P, "attn_a_sample")
            ss = (kf.reshape(NB, SN, H, HEAD_DIM), vf.reshape(NB, SN, DIFF_HEADS, DIFF_VDIM))
            w_out = a_w_out[r]
        elif m == 1:
            w = b_w_in[r].astype(BF16)
            qb, kb, vb, kl, vl = _proj_in(xp, g[0], w, nblk=nblk, kv_last=True)
            op = _attn_b_prompt(qb, kb, vb, _band_bias_tile(b_rel_bias[r]), B, S)
            sp = (kl.reshape(B, PB, H, HEAD_DIM), vl.reshape(B, PB, H, HEAD_DIM))
            qb, kb, vb, kl, vl = _proj_in(xs, g[0], w, nblk=1, kv_last=True)
            kpos = P - PB + jnp.arange(PB + SN, dtype=jnp.int32)
            kvis = _band_bias(b_rel_bias[r], pos_s, kpos)
            kvis = jnp.where((kpos >= 0)[None, None, :], kvis, NEG)
            b1 = kvis[:, :, :PB].reshape(H * SN, PB)
            b2 = kvis[:, :, PB:].reshape(H * SN, SN)
            os_ = _attn_sample(_attn_b_sample_kernel, [b1, b2], qb, kb, vb, flat(cache_b_k[r]), flat(cache_b_v[r]),
                               [], NB, SN, PB, "attn_b_sample")
            kn = kl.reshape(NB, SN, H, HEAD_DIM)
            vn = vl.reshape(NB, SN, H, HEAD_DIM)
            ss = (jnp.concatenate([cache_b_k[r], kn], axis=1)[:, SN:], jnp.concatenate([cache_b_v[r], vn], axis=1)[:, SN:])
            w_out = b_w_out[r]
        elif m == 2:
            w = c_w_in[r].astype(BF16)
            qb, kf, kb, vf, vb = _proj_in(xp, g[0], w, nblk=nblk, scale_q=SCALE)
            op = _attn_c_prompt(qb, kb, vb, tri_strict, B, S)
            sp = (kf.reshape(B, S, H, HEAD_DIM), vf.reshape(B, S, H, HEAD_DIM))
            qb, kf, kb, vf, vb = _proj_in(xs, g[0], w, nblk=1, scale_q=SCALE)
            os_ = _attn_sample(_attn_c_sample_kernel, [_tri(SN, lambda r_, c_: r_ > c_), tri_strict], qb, kb, vb,
                               flat(cache_c_k[r]), flat(cache_c_v[r]), [], NB, SN, P, "attn_c_sample")
            ss = (kf.reshape(NB, SN, H, HEAD_DIM), vf.reshape(NB, SN, H, HEAD_DIM))
            w_out = c_w_out[r]
        else:
            w_all = d_w_in[r]
            w = w_all[:, :3 * D_MODEL].astype(BF16)
            wf = jnp.pad(w_all[:, 3 * D_MODEL:], ((0, 0), (0, LANES - H))).astype(BF16)
            bf = jnp.pad(d_b_f[r].astype(F32), (0, LANES - H)).reshape(1, LANES)
            head_of_col = (jnp.arange(D_MODEL)[:, None] // HEAD_DIM == jnp.arange(LANES)[None, :]).astype(BF16)
            fox = (wf, bf, _tri(tm, lambda r_, c_: r_ >= c_), head_of_col)
            qb, kf, kb, vf, vb, lf, c, norms = _proj_in(xp, g[0], w, nblk=nblk, scale_q=SCALE_LOG2E, fox=fox)
            ck = _rows_per_head(c, B, S).reshape(B * H // FLASH_HEADS, FLASH_HEADS, S)
            op = _attn_d_prompt(qb, kb, vb, c, ck, _first_live_tile(norms, c, B, S), B, S)
            sp = (kf.reshape(B, S, H, HEAD_DIM), vf.reshape(B, S, H, HEAD_DIM), lf[:, :H].reshape(B, S, H))
            qb, kf, kb, vf, vb, lf, c, _ = _proj_in(xs, g[0], w, nblk=1, scale_q=SCALE_LOG2E, fox=fox)
            lfp = jnp.transpose(cache_d_logf[r].astype(F32), (0, 2, 1))
            lfn = _rows_per_head(lf, NB, SN).reshape(NB, H, SN)
            triu = [_tri(P, lambda r_, c_: r_ <= c_), _tri(SN, lambda r_, c_: r_ <= c_)]
            os_ = _attn_sample(_attn_d_sample_kernel, triu, qb, kb, vb, flat(cache_d_k[r]), flat(cache_d_v[r]),
                               [lfp, lfn], NB, SN, P, "attn_d_sample")
            ss = (kf.reshape(NB, SN, H, HEAD_DIM), vf.reshape(NB, SN, H, HEAD_DIM), lf[:, :H].reshape(NB, SN, H))
            w_out = d_w_out[r]
        st[m][0].append(sp)
        st[m][1].append(ss)
        wo, wu, wd = w_out.astype(BF16), w_up[i].astype(BF16), w_down[i].astype(BF16)
        xp = _out_mlp(op, xp, g[1:4], wo, wu, wd)
        xs = _out_mlp(os_, xs, g[1:4], wo, wu, wd)

    stack = lambda states, j: jnp.stack([s[j] for s in states], axis=0)
    pa, sa = st[0]
    pb, sb = st[1]
    pc, sc = st[2]
    pd, sd = st[3]
    return (xp.reshape(B, S, D_MODEL), xs.reshape(NB, SN, D_MODEL),
            stack(pa, 0), stack(pa, 1), stack(sa, 0), stack(sa, 1),
            stack(pb, 0), stack(pb, 1), stack(sb, 0), stack(sb, 1),
            stack(pc, 0), stack(pc, 1), stack(sc, 0), stack(sc, 1),
            stack(pd, 0), stack(pd, 1), stack(pd, 2), stack(sd, 0), stack(sd, 1), stack(sd, 2))
```
